```python
import jax, jax.numpy as jnp
from jax import lax
import numpy as np

D_MODEL = 1024
BATCH = 4
SEQ = 4096
DEPTH = 1
DEC_BATCH = 16
DEC_SEQ = 32
PAST_LEN = 2048

CHUNK = 64
N_META = 16
D_CONV = D_MODEL
CONV_K = 31
N_HEADS = 16
HEAD_DIM = D_MODEL // N_HEADS
D_ATTN = N_HEADS * HEAD_DIM
D_FF = 4 * D_MODEL
Q_BLOCK = 128
EPS = 1e-6
ATTN_SCALE = HEAD_DIM ** -0.5
OFF_Q = 2 * D_CONV
OFF_K = OFF_Q + D_ATTN
OFF_V = OFF_K + D_ATTN
OFF_G = OFF_V + D_ATTN
D_IN = OFF_G + 2 * D_MODEL

kernel_name = "stickbreak_conformer_hybrid_step"


def _rmsnorm(x, g):
    xf = x.astype(jnp.float32)
    y = xf * lax.rsqrt(jnp.mean(xf * xf, axis=-1, keepdims=True) + EPS)
    return (y * g.astype(jnp.float32)).astype(x.dtype)


def _layernorm(x, g, b):
    xf = x.astype(jnp.float32)
    xc = xf - jnp.mean(xf, axis=-1, keepdims=True)
    var = jnp.mean(xc * xc, axis=-1, keepdims=True)
    y = xc * lax.rsqrt(var + EPS) * g.astype(jnp.float32) + b.astype(jnp.float32)
    return y.astype(x.dtype)


def _in_proj(h, w_in):
    p = h @ w_in
    lead = p.shape[:-1]
    glu = p[..., :D_CONV] * jax.nn.sigmoid(p[..., D_CONV:OFF_Q])
    q = p[..., OFF_Q:OFF_K].reshape(*lead, N_HEADS, HEAD_DIM)
    k = p[..., OFF_K:OFF_V].reshape(*lead, N_HEADS, HEAD_DIM)
    v = p[..., OFF_V:OFF_G].reshape(*lead, N_HEADS, HEAD_DIM)
    return glu, q, k, v, p[..., OFF_G:]


def _conv_branch(ctx, w_dw, b_dw, g_ln, b_ln, w_pw2):
    y = lax.conv_general_dilated(ctx, w_dw[:, None, :].astype(ctx.dtype), window_strides=(1,), padding='VALID',
                                 dimension_numbers=('NWC', 'WIO', 'NWC'), feature_group_count=D_CONV)
    y = y + b_dw
    return jax.nn.silu(_layernorm(y, g_ln, b_ln)) @ w_pw2


def _stick_breaking(q, k, v, q_pos, k_pos):
    z = jnp.einsum('bqhd,bkhd->bhqk', q.astype(jnp.float32), k.astype(jnp.float32)) * ATTN_SCALE
    mask = k_pos[None, :] < q_pos[:, None]
    log_1m = jnp.where(mask, jax.nn.log_sigmoid(-z), 0.0)
    tail = lax.cumsum(log_1m, axis=3, reverse=True) - log_1m
    a = jnp.where(mask, jnp.exp(jax.nn.log_sigmoid(z) + tail), 0.0)
    return jnp.einsum('bhqk,bkhd->bqhd', a, v.astype(jnp.float32)).astype(v.dtype)


def _sb_prompt(q, k, v):
    B, T = q.shape[:2]
    n_blk = -(-T // Q_BLOCK)
    pad = n_blk * Q_BLOCK - T
    q_blocks = jnp.pad(q, ((0, 0), (0, pad), (0, 0), (0, 0)))
    q_blocks = q_blocks.reshape(B, n_blk, Q_BLOCK, N_HEADS, HEAD_DIM).transpose(1, 0, 2, 3, 4)
    k_pos = jnp.arange(T, dtype=jnp.int32)
    starts = jnp.arange(n_blk, dtype=jnp.int32) * Q_BLOCK

    def one_block(args):
        qb, s0 = args
        return _stick_breaking(qb, k, v, s0 + jnp.arange(Q_BLOCK, dtype=jnp.int32), k_pos)

    out = lax.map(one_block, (q_blocks, starts))
    out = out.transpose(1, 0, 2, 3, 4).reshape(B, n_blk * Q_BLOCK, N_HEADS, HEAD_DIM)
    return out[:, :T]


def _merge(conv_out, attn_out, gates, w_out):
    g_conv = jax.nn.sigmoid(gates[..., :D_MODEL])
    g_attn = jax.nn.sigmoid(gates[..., D_MODEL:])
    mixed = g_conv * conv_out + g_attn * attn_out.reshape(*attn_out.shape[:-2], D_ATTN)
    return mixed @ w_out


def _mlp(x, g, w_up, w_down):
    h = _rmsnorm(x, g)
    return jnp.square(jax.nn.relu(h @ w_up)) @ w_down


def _prompt_layer(x, g_mix, w_in, w_dw, b_dw, g_ln, b_ln, w_pw2, w_out, g_mlp, w_up, w_down):
    glu, q, k, v, gates = _in_proj(_rmsnorm(x, g_mix), w_in)
    ctx = jnp.pad(glu, ((0, 0), (CONV_K - 1, 0), (0, 0)))
    conv_out = _conv_branch(ctx, w_dw, b_dw, g_ln, b_ln, w_pw2)
    attn_out = _sb_prompt(q, k, v)
    x = x + _merge(conv_out, attn_out, gates, w_out)
    x = x + _mlp(x, g_mlp, w_up, w_down)
    return x, k, v, glu[:, -(CONV_K - 1):]


def _sample_layer(x, meta_x, cache_k, cache_v, cache_conv,
                  g_mix, w_in, w_dw, b_dw, g_ln, b_ln, w_pw2, w_out, g_mlp, w_up, w_down):
    B, S, _ = x.shape
    P = cache_k.shape[1]
    glu, q, k, v, gates = _in_proj(_rmsnorm(x, g_mix), w_in)
    ctx = jnp.concatenate([cache_conv.astype(glu.dtype), glu], axis=1)
    conv_out = _conv_branch(ctx, w_dw, b_dw, g_ln, b_ln, w_pw2)
    kv_meta = _rmsnorm(meta_x, g_mix) @ w_in[:, OFF_K:OFF_G]
    k_meta = jnp.broadcast_to(kv_meta[:, :D_ATTN].reshape(1, N_META, N_HEADS, HEAD_DIM), (B, N_META, N_HEADS, HEAD_DIM))
    v_meta = jnp.broadcast_to(kv_meta[:, D_ATTN:].reshape(1, N_META, N_HEADS, HEAD_DIM), (B, N_META, N_HEADS, HEAD_DIM))
    k_all = jnp.concatenate([k_meta.astype(k.dtype), cache_k.astype(k.dtype), k], axis=1)
    v_all = jnp.concatenate([v_meta.astype(v.dtype), cache_v.astype(v.dtype), v], axis=1)
    L = N_META + P + S
    q_pos = N_META + P + jnp.arange(S, dtype=jnp.int32)
    attn_out = _stick_breaking(q, k_all, v_all, q_pos, jnp.arange(L, dtype=jnp.int32))
    x = x + _merge(conv_out, attn_out, gates, w_out)
    x = x + _mlp(x, g_mlp, w_up, w_down)
    return x, k, v, ctx[:, -(CONV_K - 1):]


def setup_inputs(seed: int = 0) -> dict:
    key = jax.random.key(seed)
    ks = jax.random.split(key, 20)
    nrm = jax.random.normal
    f32 = jnp.float32
    return {
        "x_prompt": nrm(ks[0], (BATCH, SEQ, D_MODEL), f32),
        "x_sample": nrm(ks[1], (DEC_BATCH, DEC_SEQ, D_MODEL), f32),
        "cache_k": nrm(ks[2], (DEPTH, DEC_BATCH, PAST_LEN, N_HEADS, HEAD_DIM), f32),
        "cache_v": nrm(ks[3], (DEPTH, DEC_BATCH, PAST_LEN, N_HEADS, HEAD_DIM), f32),
        "cache_conv": 0.5 * nrm(ks[4], (DEPTH, DEC_BATCH, CONV_K - 1, D_CONV), f32),
        "meta": nrm(ks[5], (N_META, D_MODEL), f32),
        "g_mix": 1.0 + 0.1 * nrm(ks[6], (DEPTH, D_MODEL), f32),
        "w_in": nrm(ks[7], (DEPTH, D_MODEL, D_IN), f32) * D_MODEL ** -0.5,
        "w_dw": nrm(ks[8], (DEPTH, CONV_K, D_CONV), f32) * CONV_K ** -0.5,
        "b_dw": 0.02 * nrm(ks[9], (DEPTH, D_CONV), f32),
        "g_ln_conv": 1.0 + 0.1 * nrm(ks[10], (DEPTH, D_CONV), f32),
        "b_ln_conv": 0.02 * nrm(ks[11], (DEPTH, D_CONV), f32),
        "w_pw2": nrm(ks[12], (DEPTH, D_CONV, D_MODEL), f32) * D_CONV ** -0.5,
        "w_out": nrm(ks[13], (DEPTH, D_MODEL, D_MODEL), f32) * D_MODEL ** -0.5,
        "g_mlp": 1.0 + 0.1 * nrm(ks[14], (DEPTH, D_MODEL), f32),
        "w_up": nrm(ks[15], (DEPTH, D_MODEL, D_FF), f32) * D_MODEL ** -0.5,
        "w_down": nrm(ks[16], (DEPTH, D_FF, D_MODEL), f32) * D_FF ** -0.5,
        "g_final": 1.0 + 0.1 * nrm(ks[17], (D_MODEL,), f32),
    }


def reference(x_prompt, x_sample, cache_k, cache_v, cache_conv, meta, g_mix, w_in, w_dw, b_dw,
              g_ln_conv, b_ln_conv, w_pw2, w_out, g_mlp, w_up, w_down, g_final):
    assert x_sample.shape[1] <= CHUNK
    B = x_prompt.shape[0]
    meta_p = jnp.broadcast_to(meta.astype(x_prompt.dtype)[None], (B, N_META, D_MODEL))
    xp = jnp.concatenate([meta_p, x_prompt], axis=1)
    xs = x_sample
    meta_x = meta.astype(x_sample.dtype)
    kp, vp, cp, ks_, vs_, cs_ = [], [], [], [], [], []
    for l in range(DEPTH):
        layer_w = (g_mix[l], w_in[l], w_dw[l], b_dw[l], g_ln_conv[l], b_ln_conv[l],
                   w_pw2[l], w_out[l], g_mlp[l], w_up[l], w_down[l])
        xp, k_l, v_l, c_l = _prompt_layer(xp, *layer_w)
        kp.append(k_l); vp.append(v_l); cp.append(c_l)
        xs, k_l, v_l, c_l = _sample_layer(xs, meta_x, cache_k[l], cache_v[l], cache_conv[l], *layer_w)
        ks_.append(k_l); vs_.append(v_l); cs_.append(c_l)
        if l + 1 < DEPTH:
            meta_x = _prompt_layer(meta_x[None], *layer_w)[0][0]
    y_prompt = _rmsnorm(xp, g_final)[:, N_META:]
    y_sample = _rmsnorm(xs, g_final)
    new_k_prompt = jnp.stack(kp)
    new_v_prompt = jnp.stack(vp)
    new_conv_prompt = jnp.stack(cp)
    new_k_sample = jnp.stack(ks_)
    new_v_sample = jnp.stack(vs_)
    new_conv_sample = jnp.stack(cs_)
    return (y_prompt, y_sample, new_k_prompt, new_v_prompt, new_conv_prompt, new_k_sample, new_v_sample, new_conv_sample)
```

```python
import functools

import jax
import jax.numpy as jnp
from jax import lax
from jax.experimental import pallas as pl
from jax.experimental.pallas import tpu as pltpu

F32 = jnp.float32
BF16 = jnp.bfloat16

D_MODEL = 1024
N_META = 16
CONV_K = 31
N_HEADS = 16
HEAD_DIM = 64
D_FF = 4 * D_MODEL
EPS = 1e-6
ATTN_SCALE = HEAD_DIM ** -0.5
PAIR = 2 * HEAD_DIM
HALO = 32
KBLK = 256
MBLK = 128
VMEM_LIMIT = 56 * 1024 * 1024


def _sigmoid(x):
    return 1.0 / (1.0 + jnp.exp(-x))


def _rms(x, g):
    return x * lax.rsqrt(jnp.mean(x * x, axis=-1, keepdims=True) + EPS) * g


def _in_proj_kernel(x_ref, g_ref, w_ref, wb_ref, glu_ref, q_ref, k_ref, kb_ref, v_ref, vb_ref,
                    gates_ref, h_scr):
    j = pl.program_id(1)

    @pl.when(j == 0)
    def _():
        h_scr[...] = _rms(x_ref[...], g_ref[...]).astype(BF16)

    h = h_scr[...]
    p = jnp.dot(h, w_ref[...], preferred_element_type=F32)

    @pl.when(j == 0)
    def _():
        b = jnp.dot(h, wb_ref[...], preferred_element_type=F32)
        glu_ref[...] = p * _sigmoid(b)

    @pl.when(j == 1)
    def _():
        q_ref[...] = (p * ATTN_SCALE).astype(BF16)

    @pl.when(j == 2)
    def _():
        k_ref[...] = p
        kb_ref[...] = p.astype(BF16)

    @pl.when(j == 3)
    def _():
        v_ref[...] = p
        vb_ref[...] = p.astype(BF16)

    @pl.when(j == 4)
    def _():
        gates_ref[:, :D_MODEL] = _sigmoid(p).astype(BF16)

    @pl.when(j == 5)
    def _():
        gates_ref[:, D_MODEL:] = _sigmoid(p).astype(BF16)


def _in_proj(x, g, w_in, tm):
    m = x.shape[0]
    d = D_MODEL
    row = lambda i, j: (i, 0)
    return pl.pallas_call(
        _in_proj_kernel,
        grid=(m // tm, 6),
        in_specs=[
            pl.BlockSpec((tm, d), row),
            pl.BlockSpec((1, d), lambda i, j: (0, 0)),
            pl.BlockSpec((d, d), lambda i, j: (0, jnp.where(j == 0, 0, j + 1))),
            pl.BlockSpec((d, d), lambda i, j: (0, 1)),
        ],
        out_specs=[
            pl.BlockSpec((tm, d), row), pl.BlockSpec((tm, d), row),
            pl.BlockSpec((tm, d), row), pl.BlockSpec((tm, d), row),
            pl.BlockSpec((tm, d), row), pl.BlockSpec((tm, d), row),
            pl.BlockSpec((tm, 2 * d), row),
        ],
        out_shape=[
            jax.ShapeDtypeStruct((m, d), F32), jax.ShapeDtypeStruct((m, d), BF16),
            jax.ShapeDtypeStruct((m, d), F32), jax.ShapeDtypeStruct((m, d), BF16),
            jax.ShapeDtypeStruct((m, d), F32), jax.ShapeDtypeStruct((m, d), BF16),
            jax.ShapeDtypeStruct((m, 2 * d), BF16),
        ],
        scratch_shapes=[pltpu.VMEM((tm, d), BF16)],
        compiler_params=pltpu.CompilerParams(
            dimension_semantics=("arbitrary", "arbitrary"), vmem_limit_bytes=VMEM_LIMIT),
        name="in_proj",
    )(x, g.reshape(1, d), w_in, w_in)


def _conv_kernel(glu_ref, prev_ref, head_ref, wdw_ref, bdw_ref, gln_ref, bln_ref, wpw_ref, out_ref,
                 ext_scr, y_scr, *, tm):
    i = pl.program_id(1)
    ext_scr[0:HALO, :] = jnp.where(i == 0, head_ref[0], prev_ref[0])
    ext_scr[HALO:HALO + tm, :] = glu_ref[0]
    rc = min(tm, 64)
    first = HALO - (CONV_K - 1)
    for c in range(D_MODEL // 128):
        cs = slice(c * 128, (c + 1) * 128)
        for r in range(tm // rc):
            acc = jnp.broadcast_to(bdw_ref[:, cs], (rc, 128))
            for t in range(CONV_K):
                lo = r * rc + t + first
                acc = acc + wdw_ref[t:t + 1, cs] * ext_scr[lo:lo + rc, cs]
            y_scr[r * rc:(r + 1) * rc, cs] = acc
    y = y_scr[...]
    yc = y - jnp.mean(y, axis=-1, keepdims=True)
    var = jnp.mean(yc * yc, axis=-1, keepdims=True)
    ln = yc * lax.rsqrt(var + EPS) * gln_ref[...] + bln_ref[...]
    s = ln * _sigmoid(ln)
    out_ref[0] = jnp.dot(s.astype(BF16), wpw_ref[...], preferred_element_type=F32).astype(BF16)


def _conv_branch(glu, head, w_dw, b_dw, g_ln, b_ln, w_pw2, tm):
    b, t, d = glu.shape
    per_batch_head = head.shape[0] != 1
    const = lambda bi, i: (0, 0)
    return pl.pallas_call(
        functools.partial(_conv_kernel, tm=tm),
        grid=(b, t // tm),
        in_specs=[
            pl.BlockSpec((1, tm, d), lambda bi, i: (bi, i, 0)),
            pl.BlockSpec((1, HALO, d), lambda bi, i: (bi, jnp.maximum(i * (tm // HALO) - 1, 0), 0)),
            pl.BlockSpec((1, HALO, d), (lambda bi, i: (bi, 0, 0)) if per_batch_head else (lambda bi, i: (0, 0, 0))),
            pl.BlockSpec((CONV_K, d), const),
            pl.BlockSpec((1, d), const), pl.BlockSpec((1, d), const), pl.BlockSpec((1, d), const),
            pl.BlockSpec((d, d), const),
        ],
        out_specs=pl.BlockSpec((1, tm, d), lambda bi, i: (bi, i, 0)),
        out_shape=jax.ShapeDtypeStruct((b, t, d), BF16),
        scratch_shapes=[pltpu.VMEM((HALO + tm, d), F32), pltpu.VMEM((tm, d), F32)],
        compiler_params=pltpu.CompilerParams(
            dimension_semantics=("arbitrary", "arbitrary"), vmem_limit_bytes=VMEM_LIMIT),
        name="conv_branch",
    )(glu, glu, head, w_dw, b_dw.reshape(1, d), g_ln.reshape(1, d), b_ln.reshape(1, d), w_pw2)


def _stack_heads(q):
    lane = lax.broadcasted_iota(jnp.int32, q.shape, 1)
    zero = jnp.zeros_like(q)
    return jnp.concatenate([jnp.where(lane < HEAD_DIM, q, zero), jnp.where(lane >= HEAD_DIM, q, zero)], axis=0)


def _unstack_heads(acc):
    r = acc.shape[0] // 2
    lane = lax.broadcasted_iota(jnp.int32, (r, PAIR), 1)
    return jnp.where(lane < HEAD_DIM, acc[:r], acc[r:])


def _sb_block(qs, k, v, u2, mask, r_scr, acc_scr):
    z = lax.dot_general(qs, k, (((1,), (1,)), ((), ())), preferred_element_type=F32)
    sp = jnp.maximum(z, 0.0) + jnp.log(1.0 + jnp.exp(-jnp.abs(z)))
    if mask is not None:
        sp = jnp.where(mask, sp, 0.0)
    hi = sp.astype(BF16)
    lo = (sp - hi.astype(F32)).astype(BF16)
    cs = jnp.dot(jnp.concatenate([hi, lo], axis=1), u2, preferred_element_type=F32)
    r = r_scr[...]
    w = z.shape[1]
    rr = r if w == PAIR else jnp.concatenate([r] * (w // PAIR), axis=1)
    a = jnp.exp(z - cs - rr)
    if mask is not None:
        a = jnp.where(mask, a, 0.0)
    acc_scr[...] += jnp.dot(a.astype(BF16), v, preferred_element_type=F32)
    r_scr[...] = r + jnp.broadcast_to(cs[:, 0:1], r.shape)


def _causal_mask(rows_per_head, width):
    row = lax.broadcasted_iota(jnp.int32, (2 * rows_per_head, width), 0)
    col = lax.broadcasted_iota(jnp.int32, (2 * rows_per_head, width), 1)
    return col < jnp.where(row >= rows_per_head, row - rows_per_head, row)


def _meta_mask(rows, width):
    return lax.broadcasted_iota(jnp.int32, (rows, width), 1) < N_META


def _attn_prompt_kernel(q_ref, k_ref, v_ref, km_ref, vm_ref, u256_ref, u128_ref, o_ref, r_scr, acc_scr):
    qi = pl.program_id(2)
    tq = KBLK
    qs = _stack_heads(q_ref[0])
    r_scr[...] = jnp.zeros_like(r_scr)
    acc_scr[...] = jnp.zeros_like(acc_scr)
    u256 = u256_ref[...]

    d0 = pl.multiple_of(qi * tq, tq)
    _sb_block(qs, k_ref[0, pl.ds(d0, tq), :], v_ref[0, pl.ds(d0, tq), :], u256,
              _causal_mask(tq, tq), r_scr, acc_scr)

    def body(s, carry):
        start = pl.multiple_of((qi - 1 - s) * tq, tq)
        _sb_block(qs, k_ref[0, pl.ds(start, tq), :], v_ref[0, pl.ds(start, tq), :], u256,
                  None, r_scr, acc_scr)
        return carry

    lax.fori_loop(0, qi, body, 0)

    _sb_block(qs, km_ref[...], vm_ref[...], u128_ref[...], _meta_mask(2 * tq, MBLK), r_scr, acc_scr)
    o_ref[0] = _unstack_heads(acc_scr[...]).astype(BF16)


def _suffix_sum_matrix(w):
    j = lax.broadcasted_iota(jnp.int32, (w, w), 0)
    s = lax.broadcasted_iota(jnp.int32, (w, w), 1)
    u = (j >= s).astype(BF16)
    return jnp.concatenate([u, u], axis=0)


def _attn_prompt(q, k, v, k_meta, v_meta):
    b, t, d = q.shape
    tq = KBLK
    const2 = lambda bi, hp, qi: (0, 0)
    return pl.pallas_call(
        _attn_prompt_kernel,
        grid=(b, d // PAIR, t // tq),
        in_specs=[
            pl.BlockSpec((1, tq, PAIR), lambda bi, hp, qi: (bi, qi, hp)),
            pl.BlockSpec((1, t, PAIR), lambda bi, hp, qi: (bi, 0, hp)),
            pl.BlockSpec((1, t, PAIR), lambda bi, hp, qi: (bi, 0, hp)),
            pl.BlockSpec((MBLK, PAIR), lambda bi, hp, qi: (0, hp)),
            pl.BlockSpec((MBLK, PAIR), lambda bi, hp, qi: (0, hp)),
            pl.BlockSpec((2 * KBLK, KBLK), const2),
            pl.BlockSpec((2 * MBLK, MBLK), const2),
        ],
        out_specs=pl.BlockSpec((1, tq, PAIR), lambda bi, hp, qi: (bi, qi, hp)),
        out_shape=jax.ShapeDtypeStruct((b, t, d), BF16),
        scratch_shapes=[pltpu.VMEM((2 * tq, PAIR), F32), pltpu.VMEM((2 * tq, PAIR), F32)],
        compiler_params=pltpu.CompilerParams(
            dimension_semantics=("arbitrary", "arbitrary", "arbitrary"), vmem_limit_bytes=VMEM_LIMIT),
        name="attn_prompt",
    )(q, k, v, k_meta, v_meta, _suffix_sum_matrix(KBLK), _suffix_sum_matrix(MBLK))


def _attn_sample_kernel(q_ref, kn_ref, vn_ref, ck_ref, cv_ref, km_ref, vm_ref, u256_ref, u128_ref, o_ref,
                        r_scr, acc_scr, *, s_len, n_past):
    qs = _stack_heads(q_ref[0])
    r_scr[...] = jnp.zeros_like(r_scr)
    acc_scr[...] = jnp.zeros_like(acc_scr)
    u256 = u256_ref[...]
    u128 = u128_ref[...]

    _sb_block(qs, kn_ref[0], vn_ref[0], u128, _causal_mask(s_len, MBLK), r_scr, acc_scr)

    def body(s, carry):
        start = pl.multiple_of((n_past - 1 - s) * KBLK, KBLK)
        _sb_block(qs, ck_ref[0, pl.ds(start, KBLK), :].astype(BF16), cv_ref[0, pl.ds(start, KBLK), :].astype(BF16),
                  u256, None, r_scr, acc_scr)
        return carry

    lax.fori_loop(0, n_past, body, 0)

    _sb_block(qs, km_ref[...], vm_ref[...], u128, _meta_mask(2 * s_len, MBLK), r_scr, acc_scr)
    o_ref[0] = _unstack_heads(acc_scr[...]).astype(BF16)


def _attn_sample(q, k_new, v_new, cache_k, cache_v, k_meta, v_meta):
    b, s_len, d = q.shape
    p = cache_k.shape[1]
    blk = lambda bi, hp: (bi, 0, hp)
    const2 = lambda bi, hp: (0, 0)
    return pl.pallas_call(
        functools.partial(_attn_sample_kernel, s_len=s_len, n_past=p // KBLK),
        grid=(b, d // PAIR),
        in_specs=[
            pl.BlockSpec((1, s_len, PAIR), blk),
            pl.BlockSpec((1, MBLK, PAIR), blk), pl.BlockSpec((1, MBLK, PAIR), blk),
            pl.BlockSpec((1, p, PAIR), blk), pl.BlockSpec((1, p, PAIR), blk),
            pl.BlockSpec((MBLK, PAIR), lambda bi, hp: (0, hp)),
            pl.BlockSpec((MBLK, PAIR), lambda bi, hp: (0, hp)),
            pl.BlockSpec((2 * KBLK, KBLK), const2),
            pl.BlockSpec((2 * MBLK, MBLK), const2),
        ],
        out_specs=pl.BlockSpec((1, s_len, PAIR), blk),
        out_shape=jax.ShapeDtypeStruct((b, s_len, d), BF16),
        scratch_shapes=[pltpu.VMEM((2 * s_len, PAIR), F32), pltpu.VMEM((2 * s_len, PAIR), F32)],
        compiler_params=pltpu.CompilerParams(
            dimension_semantics=("arbitrary", "arbitrary"), vmem_limit_bytes=VMEM_LIMIT),
        name="attn_sample",
    )(q, k_new, v_new, cache_k, cache_v, k_meta, v_meta, _suffix_sum_matrix(KBLK), _suffix_sum_matrix(MBLK))


def _merge_mlp_kernel(x_ref, conv_ref, attn_ref, gates_ref, wout_ref, gmlp_ref, wup_ref, wdown_ref, gfin_ref,
                      y_ref):
    g = gates_ref[...]
    mixed = (g[:, :D_MODEL].astype(F32) * conv_ref[...].astype(F32)
             + g[:, D_MODEL:].astype(F32) * attn_ref[...].astype(F32))
    x1 = x_ref[...] + jnp.dot(mixed.astype(BF16), wout_ref[...], preferred_element_type=F32)
    h = _rms(x1, gmlp_ref[...]).astype(BF16)
    acc = x1
    for c in range(D_FF // D_MODEL):
        cs = slice(c * D_MODEL, (c + 1) * D_MODEL)
        u = jnp.maximum(jnp.dot(h, wup_ref[:, cs], preferred_element_type=F32), 0.0)
        acc = acc + jnp.dot((u * u).astype(BF16), wdown_ref[cs, :], preferred_element_type=F32)
    y_ref[...] = _rms(acc, gfin_ref[...])


def _merge_mlp(x, conv, attn, gates, w_out, g_mlp, w_up, w_down, g_final, tm):
    m, d = x.shape
    row = lambda i: (i, 0)
    const = lambda i: (0, 0)
    resident = functools.partial(pl.BlockSpec, index_map=const, pipeline_mode=pl.Buffered(1))
    return pl.pallas_call(
        _merge_mlp_kernel,
        grid=(m // tm,),
        in_specs=[
            pl.BlockSpec((tm, d), row), pl.BlockSpec((tm, d), row), pl.BlockSpec((tm, d), row),
            pl.BlockSpec((tm, 2 * d), row),
            resident((d, d)), resident((1, d)), resident((d, D_FF)), resident((D_FF, d)), resident((1, d)),
        ],
        out_specs=pl.BlockSpec((tm, d), row),
        out_shape=jax.ShapeDtypeStruct((m, d), F32),
        compiler_params=pltpu.CompilerParams(
            dimension_semantics=("arbitrary",), vmem_limit_bytes=VMEM_LIMIT),
        name="merge_mlp",
    )(x, conv, attn, gates, w_out, g_mlp.reshape(1, d), w_up, w_down, g_final.reshape(1, d))


def kernel(x_prompt, x_sample, cache_k, cache_v, cache_conv, meta, g_mix, w_in, w_dw, b_dw, g_ln_conv,
           b_ln_conv, w_pw2, w_out, g_mlp, w_up, w_down, g_final):
    b, t, d = x_prompt.shape
    sb, s_len, _ = x_sample.shape
    depth, _, past, _, _ = cache_k.shape
    assert depth == 1 and d == D_MODEL and meta.shape == (N_META, d)
    assert t % 512 == 0 and past % KBLK == 0 and s_len % 16 == 0 and CONV_K - 1 <= s_len <= MBLK
    n_s = sb * s_len

    w_in_b = w_in[0].astype(BF16)
    w_pw2_b = w_pw2[0].astype(BF16)
    w_out_b = w_out[0].astype(BF16)
    w_up_b = w_up[0].astype(BF16)
    w_down_b = w_down[0].astype(BF16)

    glu_p, q_p, k_p, kb_p, v_p, vb_p, gates_p = _in_proj(x_prompt.reshape(b * t, d), g_mix[0], w_in_b, tm=512)
    x_sm = jnp.concatenate([x_sample.reshape(n_s, d), meta.astype(F32)], axis=0)
    glu_s, q_s, k_s, kb_s, v_s, vb_s, gates_s = _in_proj(x_sm, g_mix[0], w_in_b, tm=n_s + N_META)

    pad_meta = ((0, MBLK - N_META), (0, 0))
    kb_meta = jnp.pad(kb_s[n_s:], pad_meta)
    vb_meta = jnp.pad(vb_s[n_s:], pad_meta)

    conv_w = (w_dw[0], b_dw[0], g_ln_conv[0], b_ln_conv[0], w_pw2_b)
    glu_p3 = glu_p.reshape(b, t, d)
    head_p = jnp.pad(glu_s[n_s:], ((HALO - N_META, 0), (0, 0)))[None]
    conv_p = _conv_branch(glu_p3, head_p, *conv_w, tm=256)
    glu_s3 = glu_s[:n_s].reshape(sb, s_len, d)
    head_s = jnp.pad(cache_conv[0], ((0, 0), (HALO - (CONV_K - 1), 0), (0, 0)))
    conv_s = _conv_branch(glu_s3, head_s, *conv_w, tm=s_len)

    attn_p = _attn_prompt(q_p.reshape(b, t, d), kb_p.reshape(b, t, d), vb_p.reshape(b, t, d), kb_meta, vb_meta)
    pad_new = ((0, 0), (0, MBLK - s_len), (0, 0))
    attn_s = _attn_sample(
        q_s[:n_s].reshape(sb, s_len, d),
        jnp.pad(kb_s[:n_s].reshape(sb, s_len, d), pad_new), jnp.pad(vb_s[:n_s].reshape(sb, s_len, d), pad_new),
        cache_k[0].reshape(sb, past, d), cache_v[0].reshape(sb, past, d), kb_meta, vb_meta)

    mlp_w = (w_out_b, g_mlp[0], w_up_b, w_down_b, g_final)
    y_p = _merge_mlp(x_prompt.reshape(b * t, d), conv_p.reshape(b * t, d), attn_p.reshape(b * t, d), gates_p,
                     *mlp_w, tm=512)
    y_s = _merge_mlp(x_sample.reshape(n_s, d), conv_s.reshape(n_s, d), attn_s.reshape(n_s, d), gates_s[:n_s],
                     *mlp_w, tm=n_s)

    def with_meta(meta_rows, real):
        full = jnp.concatenate([jnp.broadcast_to(meta_rows[None], (b, N_META, d)), real.reshape(b, t, d)], axis=1)
        return full.reshape(1, b, N_META + t, N_HEADS, HEAD_DIM)

    return (
        y_p.reshape(b, t, d),
        y_s.reshape(sb, s_len, d),
        with_meta(k_s[n_s:], k_p),
        with_meta(v_s[n_s:], v_p),
        glu_p3[:, t - (CONV_K - 1):][None],
        k_s[:n_s].reshape(1, sb, s_len, N_HEADS, HEAD_DIM),
        v_s[:n_s].reshape(1, sb, s_len, N_HEADS, HEAD_DIM),
        glu_s3[:, s_len - (CONV_K - 1):][None],
    )
```

```python
import functools

import jax
import jax.numpy as jnp
from jax import lax
from jax.experimental import pallas as pl
from jax.experimental.pallas import tpu as pltpu

F32 = jnp.float32
BF16 = jnp.bfloat16

D_MODEL = 1024
N_META = 16
CONV_K = 31
N_HEADS = 16
HEAD_DIM = 64
D_FF = 4 * D_MODEL
EPS = 1e-6
ATTN_SCALE = HEAD_DIM ** -0.5
LOG2E = 1.4426950408889634
PAIR = 2 * HEAD_DIM
HALO = 32
KBLK = 256
MBLK = 128
MASKED = -1e30
R_DONE = 160.0
VMEM_LIMIT = 56 * 1024 * 1024


def _sigmoid(x):
    return 1.0 / (1.0 + jnp.exp(-x))


def _rms(x, g):
    return x * lax.rsqrt(jnp.mean(x * x, axis=-1, keepdims=True) + EPS) * g


def _in_proj_kernel(x_ref, g_ref, w_ref, wb_ref, glu_ref, q_ref, k_ref, kb_ref, v_ref, vb_ref,
                    gates_ref, h_scr):
    j = pl.program_id(1)

    @pl.when(j == 0)
    def _():
        h_scr[...] = _rms(x_ref[...], g_ref[...]).astype(BF16)

    h = h_scr[...]
    p = jnp.dot(h, w_ref[...], preferred_element_type=F32)

    @pl.when(j == 0)
    def _():
        b = jnp.dot(h, wb_ref[...], preferred_element_type=F32)
        glu_ref[...] = p * _sigmoid(b)

    @pl.when(j == 1)
    def _():
        q_ref[...] = (p * (ATTN_SCALE * LOG2E)).astype(BF16)

    @pl.when(j == 2)
    def _():
        k_ref[...] = p
        kb_ref[...] = p.astype(BF16)

    @pl.when(j == 3)
    def _():
        v_ref[...] = p
        vb_ref[...] = p.astype(BF16)

    @pl.when(j == 4)
    def _():
        gates_ref[:, :D_MODEL] = _sigmoid(p).astype(BF16)

    @pl.when(j == 5)
    def _():
        gates_ref[:, D_MODEL:] = _sigmoid(p).astype(BF16)


def _in_proj(x, g, w_in, tm):
    m = x.shape[0]
    d = D_MODEL
    row = lambda i, j: (i, 0)
    return pl.pallas_call(
        _in_proj_kernel,
        grid=(m // tm, 6),
        in_specs=[
            pl.BlockSpec((tm, d), row),
            pl.BlockSpec((1, d), lambda i, j: (0, 0)),
            pl.BlockSpec((d, d), lambda i, j: (0, jnp.where(j == 0, 0, j + 1))),
            pl.BlockSpec((d, d), lambda i, j: (0, 1)),
        ],
        out_specs=[
            pl.BlockSpec((tm, d), row), pl.BlockSpec((tm, d), row),
            pl.BlockSpec((tm, d), row), pl.BlockSpec((tm, d), row),
            pl.BlockSpec((tm, d), row), pl.BlockSpec((tm, d), row),
            pl.BlockSpec((tm, 2 * d), row),
        ],
        out_shape=[
            jax.ShapeDtypeStruct((m, d), F32), jax.ShapeDtypeStruct((m, d), BF16),
            jax.ShapeDtypeStruct((m, d), F32), jax.ShapeDtypeStruct((m, d), BF16),
            jax.ShapeDtypeStruct((m, d), F32), jax.ShapeDtypeStruct((m, d), BF16),
            jax.ShapeDtypeStruct((m, 2 * d), BF16),
        ],
        scratch_shapes=[pltpu.VMEM((tm, d), BF16)],
        compiler_params=pltpu.CompilerParams(
            dimension_semantics=("arbitrary", "arbitrary"), vmem_limit_bytes=VMEM_LIMIT),
        name="in_proj",
    )(x, g.reshape(1, d), w_in, w_in)


def _conv_kernel(glu_ref, prev_ref, head_ref, wdw_ref, bdw_ref, gln_ref, bln_ref, wpw_ref, out_ref,
                 ext_scr, y_scr, *, tm):
    i = pl.program_id(1)
    ext_scr[0:HALO, :] = jnp.where(i == 0, head_ref[0], prev_ref[0])
    ext_scr[HALO:HALO + tm, :] = glu_ref[0]
    rc = min(tm, 64)
    first = HALO - (CONV_K - 1)
    for c in range(D_MODEL // 128):
        cs = slice(c * 128, (c + 1) * 128)
        for r in range(tm // rc):
            acc = jnp.broadcast_to(bdw_ref[:, cs], (rc, 128))
            for t in range(CONV_K):
                lo = r * rc + t + first
                acc = acc + wdw_ref[t:t + 1, cs] * ext_scr[lo:lo + rc, cs]
            y_scr[r * rc:(r + 1) * rc, cs] = acc
    y = y_scr[...]
    yc = y - jnp.mean(y, axis=-1, keepdims=True)
    var = jnp.mean(yc * yc, axis=-1, keepdims=True)
    ln = yc * lax.rsqrt(var + EPS) * gln_ref[...] + bln_ref[...]
    s = ln * _sigmoid(ln)
    out_ref[0] = jnp.dot(s.astype(BF16), wpw_ref[...], preferred_element_type=F32).astype(BF16)


def _conv_branch(glu, head, w_dw, b_dw, g_ln, b_ln, w_pw2, tm):
    b, t, d = glu.shape
    per_batch_head = head.shape[0] != 1
    const = lambda bi, i: (0, 0)
    return pl.pallas_call(
        functools.partial(_conv_kernel, tm=tm),
        grid=(b, t // tm),
        in_specs=[
            pl.BlockSpec((1, tm, d), lambda bi, i: (bi, i, 0)),
            pl.BlockSpec((1, HALO, d), lambda bi, i: (bi, jnp.maximum(i * (tm // HALO) - 1, 0), 0)),
            pl.BlockSpec((1, HALO, d), (lambda bi, i: (bi, 0, 0)) if per_batch_head else (lambda bi, i: (0, 0, 0))),
            pl.BlockSpec((CONV_K, d), const),
            pl.BlockSpec((1, d), const), pl.BlockSpec((1, d), const), pl.BlockSpec((1, d), const),
            pl.BlockSpec((d, d), const),
        ],
        out_specs=pl.BlockSpec((1, tm, d), lambda bi, i: (bi, i, 0)),
        out_shape=jax.ShapeDtypeStruct((b, t, d), BF16),
        scratch_shapes=[pltpu.VMEM((HALO + tm, d), F32), pltpu.VMEM((tm, d), F32)],
        compiler_params=pltpu.CompilerParams(
            dimension_semantics=("arbitrary", "arbitrary"), vmem_limit_bytes=VMEM_LIMIT),
        name="conv_branch",
    )(glu, glu, head, w_dw, b_dw.reshape(1, d), g_ln.reshape(1, d), b_ln.reshape(1, d), w_pw2)


def _stack_heads(q):
    lane = lax.broadcasted_iota(jnp.int32, q.shape, 1)
    zero = jnp.zeros_like(q)
    return jnp.concatenate([jnp.where(lane < HEAD_DIM, q, zero), jnp.where(lane >= HEAD_DIM, q, zero)], axis=0)


def _unstack_heads(acc):
    r = acc.shape[0] // 2
    lane = lax.broadcasted_iota(jnp.int32, (r, PAIR), 1)
    return jnp.where(lane < HEAD_DIM, acc[:r], acc[r:])


def _sb_block(z, v, u, mask, r_scr, acc_scr):
    sp = jnp.maximum(z, 0.0) + jnp.log2(1.0 + jnp.exp2(-jnp.abs(z)))
    ls = z - sp
    if mask is not None:
        sp = jnp.where(mask, sp, 0.0)
        ls = jnp.where(mask, ls, MASKED)
    spb = sp.astype(BF16)
    cs = jnp.dot(spb, u, preferred_element_type=F32)
    r = r_scr[...]
    w = z.shape[1]
    rr = r if w == PAIR else jnp.concatenate([r] * (w // PAIR), axis=1)
    a = jnp.exp2(ls - cs - rr)
    acc_scr[...] += jnp.dot(a.astype(BF16), v, preferred_element_type=F32)
    total = cs[:, 0:1] + spb[:, 0:1].astype(F32)
    r_new = r + jnp.broadcast_to(total, r.shape)
    r_scr[...] = r_new
    return jnp.min(r_new)


def _causal_mask(rows_per_head, width):
    row = lax.broadcasted_iota(jnp.int32, (2 * rows_per_head, width), 0)
    col = lax.broadcasted_iota(jnp.int32, (2 * rows_per_head, width), 1)
    return col < jnp.where(row >= rows_per_head, row - rows_per_head, row)


def _meta_mask(rows, width):
    return lax.broadcasted_iota(jnp.int32, (rows, width), 1) < N_META


def _transpose_bf16(x):
    return x.astype(F32).T.astype(BF16)


def _attn_prompt_kernel(q_ref, k_ref, v_ref, km_ref, vm_ref, u256_ref, u128_ref, o_ref,
                        kt_scr, kmt_scr, r_scr, acc_scr):
    qi = pl.program_id(2)
    tq = KBLK

    @pl.when(qi == 0)
    def _():
        for c in range(k_ref.shape[1] // KBLK):
            kt_scr[:, c * KBLK:(c + 1) * KBLK] = _transpose_bf16(k_ref[0, c * KBLK:(c + 1) * KBLK, :])
        kmt_scr[...] = _transpose_bf16(km_ref[...])

    qs = _stack_heads(q_ref[0])
    r_scr[...] = jnp.zeros_like(r_scr)
    acc_scr[...] = jnp.zeros_like(acc_scr)
    u256 = u256_ref[...]

    def block(start, mask):
        z = jnp.dot(qs, kt_scr[:, pl.ds(start, tq)], preferred_element_type=F32)
        return _sb_block(z, v_ref[0, pl.ds(start, tq), :], u256, mask, r_scr, acc_scr)

    rmin = block(pl.multiple_of(qi * tq, tq), _causal_mask(tq, tq))

    def cond(c):
        return jnp.logical_and(c[0] < qi, c[1] < R_DONE)

    def body(c):
        return c[0] + 1, block(pl.multiple_of((qi - 1 - c[0]) * tq, tq), None)

    _, rmin = lax.while_loop(cond, body, (jnp.int32(0), rmin))

    @pl.when(rmin < R_DONE)
    def _():
        z = jnp.dot(qs, kmt_scr[...], preferred_element_type=F32)
        _sb_block(z, vm_ref[...], u128_ref[...], _meta_mask(2 * tq, MBLK), r_scr, acc_scr)

    o_ref[0] = _unstack_heads(acc_scr[...]).astype(BF16)


def _suffix_sum_matrix(w):
    j = lax.broadcasted_iota(jnp.int32, (w, w), 0)
    s = lax.broadcasted_iota(jnp.int32, (w, w), 1)
    return (j > s).astype(BF16)


def _attn_prompt(q, k, v, k_meta, v_meta):
    b, t, d = q.shape
    tq = KBLK
    const2 = lambda bi, hp, qi: (0, 0)
    return pl.pallas_call(
        _attn_prompt_kernel,
        grid=(b, d // PAIR, t // tq),
        in_specs=[
            pl.BlockSpec((1, tq, PAIR), lambda bi, hp, qi: (bi, qi, hp)),
            pl.BlockSpec((1, t, PAIR), lambda bi, hp, qi: (bi, 0, hp)),
            pl.BlockSpec((1, t, PAIR), lambda bi, hp, qi: (bi, 0, hp)),
            pl.BlockSpec((MBLK, PAIR), lambda bi, hp, qi: (0, hp)),
            pl.BlockSpec((MBLK, PAIR), lambda bi, hp, qi: (0, hp)),
            pl.BlockSpec((KBLK, KBLK), const2),
            pl.BlockSpec((MBLK, MBLK), const2),
        ],
        out_specs=pl.BlockSpec((1, tq, PAIR), lambda bi, hp, qi: (bi, qi, hp)),
        out_shape=jax.ShapeDtypeStruct((b, t, d), BF16),
        scratch_shapes=[pltpu.VMEM((PAIR, t), BF16), pltpu.VMEM((PAIR, MBLK), BF16),
                        pltpu.VMEM((2 * tq, PAIR), F32), pltpu.VMEM((2 * tq, PAIR), F32)],
        compiler_params=pltpu.CompilerParams(
            dimension_semantics=("arbitrary", "arbitrary", "arbitrary"), vmem_limit_bytes=VMEM_LIMIT),
        name="attn_prompt",
    )(q, k, v, k_meta, v_meta, _suffix_sum_matrix(KBLK), _suffix_sum_matrix(MBLK))


def _attn_sample_kernel(q_ref, kn_ref, vn_ref, ck_ref, cv_ref, km_ref, vm_ref, u256_ref, u128_ref, o_ref,
                        r_scr, acc_scr, *, s_len, n_past):
    qs = _stack_heads(q_ref[0])
    r_scr[...] = jnp.zeros_like(r_scr)
    acc_scr[...] = jnp.zeros_like(acc_scr)
    u256 = u256_ref[...]
    u128 = u128_ref[...]
    nt = (((1,), (1,)), ((), ()))

    z = lax.dot_general(qs, kn_ref[0], nt, preferred_element_type=F32)
    rmin = _sb_block(z, vn_ref[0], u128, _causal_mask(s_len, MBLK), r_scr, acc_scr)

    def cond(c):
        return jnp.logical_and(c[0] < n_past, c[1] < R_DONE)

    def body(c):
        start = pl.multiple_of((n_past - 1 - c[0]) * KBLK, KBLK)
        z = lax.dot_general(qs, ck_ref[0, pl.ds(start, KBLK), :].astype(BF16), nt, preferred_element_type=F32)
        return c[0] + 1, _sb_block(z, cv_ref[0, pl.ds(start, KBLK), :].astype(BF16), u256, None, r_scr, acc_scr)

    _, rmin = lax.while_loop(cond, body, (jnp.int32(0), rmin))

    @pl.when(rmin < R_DONE)
    def _():
        z = lax.dot_general(qs, km_ref[...], nt, preferred_element_type=F32)
        _sb_block(z, vm_ref[...], u128, _meta_mask(2 * s_len, MBLK), r_scr, acc_scr)

    o_ref[0] = _unstack_heads(acc_scr[...]).astype(BF16)


def _attn_sample(q, k_new, v_new, cache_k, cache_v, k_meta, v_meta):
    b, s_len, d = q.shape
    p = cache_k.shape[1]
    blk = lambda bi, hp: (bi, 0, hp)
    const2 = lambda bi, hp: (0, 0)
    return pl.pallas_call(
        functools.partial(_attn_sample_kernel, s_len=s_len, n_past=p // KBLK),
        grid=(b, d // PAIR),
        in_specs=[
            pl.BlockSpec((1, s_len, PAIR), blk),
            pl.BlockSpec((1, MBLK, PAIR), blk), pl.BlockSpec((1, MBLK, PAIR), blk),
            pl.BlockSpec((1, p, PAIR), blk), pl.BlockSpec((1, p, PAIR), blk),
            pl.BlockSpec((MBLK, PAIR), lambda bi, hp: (0, hp)),
            pl.BlockSpec((MBLK, PAIR), lambda bi, hp: (0, hp)),
            pl.BlockSpec((KBLK, KBLK), const2),
            pl.BlockSpec((MBLK, MBLK), const2),
        ],
        out_specs=pl.BlockSpec((1, s_len, PAIR), blk),
        out_shape=jax.ShapeDtypeStruct((b, s_len, d), BF16),
        scratch_shapes=[pltpu.VMEM((2 * s_len, PAIR), F32), pltpu.VMEM((2 * s_len, PAIR), F32)],
        compiler_params=pltpu.CompilerParams(
            dimension_semantics=("arbitrary", "arbitrary"), vmem_limit_bytes=VMEM_LIMIT),
        name="attn_sample",
    )(q, k_new, v_new, cache_k, cache_v, k_meta, v_meta, _suffix_sum_matrix(KBLK), _suffix_sum_matrix(MBLK))


def _merge_mlp_kernel(x_ref, conv_ref, attn_ref, gates_ref, wout_ref, gmlp_ref, wup_ref, wdown_ref, gfin_ref,
                      y_ref):
    g = gates_ref[...]
    mixed = (g[:, :D_MODEL].astype(F32) * conv_ref[...].astype(F32)
             + g[:, D_MODEL:].astype(F32) * attn_ref[...].astype(F32))
    x1 = x_ref[...] + jnp.dot(mixed.astype(BF16), wout_ref[...], preferred_element_type=F32)
    h = _rms(x1, gmlp_ref[...]).astype(BF16)
    acc = x1
    for c in range(D_FF // D_MODEL):
        cs = slice(c * D_MODEL, (c + 1) * D_MODEL)
        u = jnp.maximum(jnp.dot(h, wup_ref[:, cs], preferred_element_type=F32), 0.0)
        acc = acc + jnp.dot((u * u).astype(BF16), wdown_ref[cs, :], preferred_element_type=F32)
    y_ref[...] = _rms(acc, gfin_ref[...])


def _merge_mlp(x, conv, attn, gates, w_out, g_mlp, w_up, w_down, g_final, tm):
    m, d = x.shape
    row = lambda i: (i, 0)
    const = lambda i: (0, 0)
    resident = functools.partial(pl.BlockSpec, index_map=const, pipeline_mode=pl.Buffered(1))
    return pl.pallas_call(
        _merge_mlp_kernel,
        grid=(m // tm,),
        in_specs=[
            pl.BlockSpec((tm, d), row), pl.BlockSpec((tm, d), row), pl.BlockSpec((tm, d), row),
            pl.BlockSpec((tm, 2 * d), row),
            resident((d, d)), resident((1, d)), resident((d, D_FF)), resident((D_FF, d)), resident((1, d)),
        ],
        out_specs=pl.BlockSpec((tm, d), row),
        out_shape=jax.ShapeDtypeStruct((m, d), F32),
        compiler_params=pltpu.CompilerParams(
            dimension_semantics=("arbitrary",), vmem_limit_bytes=VMEM_LIMIT),
        name="merge_mlp",
    )(x, conv, attn, gates, w_out, g_mlp.reshape(1, d), w_up, w_down, g_final.reshape(1, d))


def kernel(x_prompt, x_sample, cache_k, cache_v, cache_conv, meta, g_mix, w_in, w_dw, b_dw, g_ln_conv,
           b_ln_conv, w_pw2, w_out, g_mlp, w_up, w_down, g_final):
    b, t, d = x_prompt.shape
    sb, s_len, _ = x_sample.shape
    depth, _, past, _, _ = cache_k.shape
    assert depth == 1 and d == D_MODEL and meta.shape == (N_META, d)
    assert t % 512 == 0 and past % KBLK == 0 and s_len % 16 == 0 and CONV_K - 1 <= s_len <= MBLK
    n_s = sb * s_len

    w_in_b = w_in[0].astype(BF16)
    w_pw2_b = w_pw2[0].astype(BF16)
    w_out_b = w_out[0].astype(BF16)
    w_up_b = w_up[0].astype(BF16)
    w_down_b = w_down[0].astype(BF16)

    glu_p, q_p, k_p, kb_p, v_p, vb_p, gates_p = _in_proj(x_prompt.reshape(b * t, d), g_mix[0], w_in_b, tm=512)
    x_sm = jnp.concatenate([x_sample.reshape(n_s, d), meta.astype(F32)], axis=0)
    glu_s, q_s, k_s, kb_s, v_s, vb_s, gates_s = _in_proj(x_sm, g_mix[0], w_in_b, tm=n_s + N_META)

    pad_meta = ((0, MBLK - N_META), (0, 0))
    kb_meta = jnp.pad(kb_s[n_s:], pad_meta)
    vb_meta = jnp.pad(vb_s[n_s:], pad_meta)

    conv_w = (w_dw[0], b_dw[0], g_ln_conv[0], b_ln_conv[0], w_pw2_b)
    glu_p3 = glu_p.reshape(b, t, d)
    head_p = jnp.pad(glu_s[n_s:], ((HALO - N_META, 0), (0, 0)))[None]
    conv_p = _conv_branch(glu_p3, head_p, *conv_w, tm=256)
    glu_s3 = glu_s[:n_s].reshape(sb, s_len, d)
    head_s = jnp.pad(cache_conv[0], ((0, 0), (HALO - (CONV_K - 1), 0), (0, 0)))
    conv_s = _conv_branch(glu_s3, head_s, *conv_w, tm=s_len)

    attn_p = _attn_prompt(q_p.reshape(b, t, d), kb_p.reshape(b, t, d), vb_p.reshape(b, t, d), kb_meta, vb_meta)
    pad_new = ((0, 0), (0, MBLK - s_len), (0, 0))
    attn_s = _attn_sample(
        q_s[:n_s].reshape(sb, s_len, d),
        jnp.pad(kb_s[:n_s].reshape(sb, s_len, d), pad_new), jnp.pad(vb_s[:n_s].reshape(sb, s_len, d), pad_new),
        cache_k[0].reshape(sb, past, d), cache_v[0].reshape(sb, past, d), kb_meta, vb_meta)

    mlp_w = (w_out_b, g_mlp[0], w_up_b, w_down_b, g_final)
    y_p = _merge_mlp(x_prompt.reshape(b * t, d), conv_p.reshape(b * t, d), attn_p.reshape(b * t, d), gates_p,
                     *mlp_w, tm=512)
    y_s = _merge_mlp(x_sample.reshape(n_s, d), conv_s.reshape(n_s, d), attn_s.reshape(n_s, d), gates_s[:n_s],
                     *mlp_w, tm=n_s)

    def with_meta(meta_rows, real):
        full = jnp.concatenate([jnp.broadcast_to(meta_rows[None], (b, N_META, d)), real.reshape(b, t, d)], axis=1)
        return full.reshape(1, b, N_META + t, N_HEADS, HEAD_DIM)

    return (
        y_p.reshape(b, t, d),
        y_s.reshape(sb, s_len, d),
        with_meta(k_s[n_s:], k_p),
        with_meta(v_s[n_s:], v_p),
        glu_p3[:, t - (CONV_K - 1):][None],
        k_s[:n_s].reshape(1, sb, s_len, N_HEADS, HEAD_DIM),
        v_s[:n_s].reshape(1, sb, s_len, N_HEADS, HEAD_DIM),
        glu_s3[:, s_len - (CONV_K - 1):][None],
    )
```

```python
import functools

import jax
import jax.numpy as jnp
from jax import lax
from jax.experimental import pallas as pl
from jax.experimental.pallas import tpu as pltpu

F32 = jnp.float32
BF16 = jnp.bfloat16

D_MODEL = 1024
N_META = 16
CONV_K = 31
N_HEADS = 16
HEAD_DIM = 64
D_FF = 4 * D_MODEL
EPS = 1e-6
ATTN_SCALE = HEAD_DIM ** -0.5
LOG2E = 1.4426950408889634
PAIR = 2 * HEAD_DIM
SUBLANES = 8
HALO = 32
KBLK = 256
MBLK = 128
MASKED = -1e30
R_DONE = 160.0
VMEM_LIMIT = 56 * 1024 * 1024


def _sigmoid(x):
    return 1.0 / (1.0 + jnp.exp(-x))


def _rms(x, g):
    return x * lax.rsqrt(jnp.mean(x * x, axis=-1, keepdims=True) + EPS) * g


def _in_proj_kernel(x_ref, g_ref, w_ref, wb_ref, glu_ref, q_ref, k_ref, kb_ref, v_ref, vb_ref,
                    gates_ref, h_scr):
    j = pl.program_id(1)

    @pl.when(j == 0)
    def _():
        h_scr[...] = _rms(x_ref[...], g_ref[...]).astype(BF16)

    h = h_scr[...]
    p = jnp.dot(h, w_ref[...], preferred_element_type=F32)

    @pl.when(j == 0)
    def _():
        b = jnp.dot(h, wb_ref[...], preferred_element_type=F32)
        glu_ref[...] = p * _sigmoid(b)

    @pl.when(j == 1)
    def _():
        q_ref[...] = (p * (ATTN_SCALE * LOG2E)).astype(BF16)

    @pl.when(j == 2)
    def _():
        k_ref[...] = p
        kb_ref[...] = p.astype(BF16)

    @pl.when(j == 3)
    def _():
        v_ref[...] = p
        vb_ref[...] = p.astype(BF16)

    @pl.when(j == 4)
    def _():
        gates_ref[:, :D_MODEL] = _sigmoid(p).astype(BF16)

    @pl.when(j == 5)
    def _():
        gates_ref[:, D_MODEL:] = _sigmoid(p).astype(BF16)


def _in_proj(x, g, w_in, tm):
    m = x.shape[0]
    d = D_MODEL
    row = lambda i, j: (i, 0)
    return pl.pallas_call(
        _in_proj_kernel,
        grid=(m // tm, 6),
        in_specs=[
            pl.BlockSpec((tm, d), row),
            pl.BlockSpec((1, d), lambda i, j: (0, 0)),
            pl.BlockSpec((d, d), lambda i, j: (0, jnp.where(j == 0, 0, j + 1))),
            pl.BlockSpec((d, d), lambda i, j: (0, 1)),
        ],
        out_specs=[
            pl.BlockSpec((tm, d), row), pl.BlockSpec((tm, d), row),
            pl.BlockSpec((tm, d), row), pl.BlockSpec((tm, d), row),
            pl.BlockSpec((tm, d), row), pl.BlockSpec((tm, d), row),
            pl.BlockSpec((tm, 2 * d), row),
        ],
        out_shape=[
            jax.ShapeDtypeStruct((m, d), F32), jax.ShapeDtypeStruct((m, d), BF16),
            jax.ShapeDtypeStruct((m, d), F32), jax.ShapeDtypeStruct((m, d), BF16),
            jax.ShapeDtypeStruct((m, d), F32), jax.ShapeDtypeStruct((m, d), BF16),
            jax.ShapeDtypeStruct((m, 2 * d), BF16),
        ],
        scratch_shapes=[pltpu.VMEM((tm, d), BF16)],
        compiler_params=pltpu.CompilerParams(
            dimension_semantics=("arbitrary", "arbitrary"), vmem_limit_bytes=VMEM_LIMIT),
        name="in_proj",
    )(x, g.reshape(1, d), w_in, w_in)


def _in_proj_prompt_kernel(x_ref, meta_ref, g_ref, w_ref, wt_ref, glu_ref, q_ref, gates_ref, kt_ref, vt_ref,
                           h_scr, hs_scr, carry_scr, *, tm, n_tiles):
    t = pl.program_id(1)
    d = D_MODEL
    nt = (((1,), (1,)), ((), ()))

    @pl.when(t == 0)
    def _():
        carry_scr[...] = _rms(meta_ref[...], g_ref[...]).astype(BF16)

    @pl.when(t < n_tiles)
    def _():
        h_scr[...] = _rms(x_ref[0], g_ref[...]).astype(BF16)

    hs_scr[0:N_META, :] = carry_scr[...]
    hs_scr[N_META:tm, :] = h_scr[0:tm - N_META, :]
    carry_scr[...] = h_scr[tm - N_META:tm, :]

    @pl.when(t < n_tiles)
    def _():
        h = h_scr[...]
        proj = lambda c: jnp.dot(h, w_ref[:, c * d:(c + 1) * d], preferred_element_type=F32)
        glu_ref[0] = proj(0) * _sigmoid(proj(1))
        q_ref[0] = (proj(2) * (ATTN_SCALE * LOG2E)).astype(BF16)
        gates_ref[0, :, :d] = _sigmoid(proj(3)).astype(BF16)
        gates_ref[0, :, d:] = _sigmoid(proj(4)).astype(BF16)

    hs = hs_scr[...]
    kt_ref[0] = lax.dot_general(wt_ref[0:d, :], hs, nt, preferred_element_type=F32)
    vt_ref[0] = lax.dot_general(wt_ref[d:2 * d, :], hs, nt, preferred_element_type=F32)


def _in_proj_prompt(x, meta, g, w_main, w_kvt, tm):
    b, t_len, d = x.shape
    n_tiles = t_len // tm
    row = lambda bi, t: (bi, jnp.minimum(t, n_tiles - 1), 0)
    const = lambda bi, t: (0, 0)
    resident = functools.partial(pl.BlockSpec, index_map=const, pipeline_mode=pl.Buffered(1))
    return pl.pallas_call(
        functools.partial(_in_proj_prompt_kernel, tm=tm, n_tiles=n_tiles),
        grid=(b, n_tiles + 1),
        in_specs=[
            pl.BlockSpec((1, tm, d), row),
            resident((N_META, d)), resident((1, d)), resident((d, 5 * d)), resident((2 * d, d)),
        ],
        out_specs=[
            pl.BlockSpec((1, tm, d), row), pl.BlockSpec((1, tm, d), row), pl.BlockSpec((1, tm, 2 * d), row),
            pl.BlockSpec((1, d, tm), lambda bi, t: (bi, 0, t)),
            pl.BlockSpec((1, d, tm), lambda bi, t: (bi, 0, t)),
        ],
        out_shape=[
            jax.ShapeDtypeStruct((b, t_len, d), F32), jax.ShapeDtypeStruct((b, t_len, d), BF16),
            jax.ShapeDtypeStruct((b, t_len, 2 * d), BF16),
            jax.ShapeDtypeStruct((b, d, N_META + t_len), F32), jax.ShapeDtypeStruct((b, d, N_META + t_len), F32),
        ],
        scratch_shapes=[pltpu.VMEM((tm, d), BF16), pltpu.VMEM((tm, d), BF16), pltpu.VMEM((N_META, d), BF16)],
        compiler_params=pltpu.CompilerParams(
            dimension_semantics=("arbitrary", "arbitrary"), vmem_limit_bytes=VMEM_LIMIT),
        name="in_proj_prompt",
    )(x, meta, g.reshape(1, d), w_main, w_kvt)


def _conv_kernel(glu_ref, prev_ref, head_ref, wdw_ref, bdw_ref, gln_ref, bln_ref, wpw_ref, out_ref,
                 ext_scr, y_scr, *, tm):
    i = pl.program_id(1)
    ext_scr[0, 0:HALO, :] = jnp.where(i == 0, head_ref[0], prev_ref[0])
    ext_scr[0, HALO:HALO + tm, :] = glu_ref[0]
    shifted_rows = HALO + tm - SUBLANES
    for m in range(1, SUBLANES):
        ext_scr[m, 0:shifted_rows, :] = ext_scr[0, m:m + shifted_rows, :]
    rc = min(tm, 64)
    first = HALO - (CONV_K - 1)
    for c in range(D_MODEL // 128):
        cs = slice(c * 128, (c + 1) * 128)
        for r in range(tm // rc):
            acc = jnp.broadcast_to(bdw_ref[:, cs], (rc, 128))
            for t in range(CONV_K):
                m = (t + first) % SUBLANES
                lo = r * rc + (t + first) - m
                acc = acc + wdw_ref[t:t + 1, cs] * ext_scr[m, lo:lo + rc, cs]
            y_scr[r * rc:(r + 1) * rc, cs] = acc
    y = y_scr[...]
    yc = y - jnp.mean(y, axis=-1, keepdims=True)
    var = jnp.mean(yc * yc, axis=-1, keepdims=True)
    ln = yc * lax.rsqrt(var + EPS) * gln_ref[...] + bln_ref[...]
    s = ln * _sigmoid(ln)
    out_ref[0] = jnp.dot(s.astype(BF16), wpw_ref[...], preferred_element_type=F32).astype(BF16)


def _conv_branch(glu, head, w_dw, b_dw, g_ln, b_ln, w_pw2, tm):
    b, t, d = glu.shape
    per_batch_head = head.shape[0] != 1
    const = lambda bi, i: (0, 0)
    return pl.pallas_call(
        functools.partial(_conv_kernel, tm=tm),
        grid=(b, t // tm),
        in_specs=[
            pl.BlockSpec((1, tm, d), lambda bi, i: (bi, i, 0)),
            pl.BlockSpec((1, HALO, d), lambda bi, i: (bi, jnp.maximum(i * (tm // HALO) - 1, 0), 0)),
            pl.BlockSpec((1, HALO, d), (lambda bi, i: (bi, 0, 0)) if per_batch_head else (lambda bi, i: (0, 0, 0))),
            pl.BlockSpec((CONV_K, d), const),
            pl.BlockSpec((1, d), const), pl.BlockSpec((1, d), const), pl.BlockSpec((1, d), const),
            pl.BlockSpec((d, d), const),
        ],
        out_specs=pl.BlockSpec((1, tm, d), lambda bi, i: (bi, i, 0)),
        out_shape=jax.ShapeDtypeStruct((b, t, d), BF16),
        scratch_shapes=[pltpu.VMEM((SUBLANES, HALO + tm, d), F32), pltpu.VMEM((tm, d), F32)],
        compiler_params=pltpu.CompilerParams(
            dimension_semantics=("arbitrary", "arbitrary"), vmem_limit_bytes=VMEM_LIMIT),
        name="conv_branch",
    )(glu, glu, head, w_dw, b_dw.reshape(1, d), g_ln.reshape(1, d), b_ln.reshape(1, d), w_pw2)


def _stack_heads(q):
    lane = lax.broadcasted_iota(jnp.int32, q.shape, 1)
    zero = jnp.zeros_like(q)
    return jnp.concatenate([jnp.where(lane < HEAD_DIM, q, zero), jnp.where(lane >= HEAD_DIM, q, zero)], axis=0)


def _unstack_heads(acc):
    r = acc.shape[0] // 2
    lane = lax.broadcasted_iota(jnp.int32, (r, PAIR), 1)
    return jnp.where(lane < HEAD_DIM, acc[:r], acc[r:])


def _sb_block(z, v, u, mask, r_scr, acc_scr):
    sp = jnp.maximum(z, 0.0) + jnp.log2(1.0 + jnp.exp2(-jnp.abs(z)))
    ls = z - sp
    if mask is not None:
        sp = jnp.where(mask, sp, 0.0)
        ls = jnp.where(mask, ls, MASKED)
    spb = sp.astype(BF16)
    cs = jnp.dot(spb, u, preferred_element_type=F32)
    r = r_scr[...]
    w = z.shape[1]
    rr = r if w == PAIR else jnp.concatenate([r] * (w // PAIR), axis=1)
    a = jnp.exp2(ls - cs - rr)
    acc_scr[...] += jnp.dot(a.astype(BF16), v, preferred_element_type=F32)
    total = cs[:, 0:1] + spb[:, 0:1].astype(F32)
    r_new = r + jnp.broadcast_to(total, r.shape)
    r_scr[...] = r_new
    return jnp.min(r_new)


def _causal_mask(rows_per_head, width):
    row = lax.broadcasted_iota(jnp.int32, (2 * rows_per_head, width), 0)
    col = lax.broadcasted_iota(jnp.int32, (2 * rows_per_head, width), 1)
    return col < jnp.where(row >= rows_per_head, row - rows_per_head, row)


def _meta_mask(rows, width):
    return lax.broadcasted_iota(jnp.int32, (rows, width), 1) < N_META


def _attn_prompt_kernel(q_ref, kt_ref, vt_ref, u256_ref, u128_ref, o_ref,
                        kt_scr, v_scr, kmt_scr, vm_scr, r_scr, acc_scr):
    qi = pl.program_id(2)
    tq = KBLK

    @pl.when(qi == 0)
    def _():
        for c in range(v_scr.shape[0] // KBLK):
            cols = slice(N_META + c * KBLK, N_META + (c + 1) * KBLK)
            kt_scr[:, c * KBLK:(c + 1) * KBLK] = kt_ref[0, :, cols].astype(BF16)
            v_scr[c * KBLK:(c + 1) * KBLK, :] = vt_ref[0, :, cols].T.astype(BF16)
        kmt_scr[...] = kt_ref[0, :, 0:MBLK].astype(BF16)
        vm_scr[...] = vt_ref[0, :, 0:MBLK].T.astype(BF16)

    qs = _stack_heads(q_ref[0])
    r_scr[...] = jnp.zeros_like(r_scr)
    acc_scr[...] = jnp.zeros_like(acc_scr)
    u256 = u256_ref[...]

    def block(start, mask):
        z = jnp.dot(qs, kt_scr[:, pl.ds(start, tq)], preferred_element_type=F32)
        return _sb_block(z, v_scr[pl.ds(start, tq), :], u256, mask, r_scr, acc_scr)

    rmin = block(pl.multiple_of(qi * tq, tq), _causal_mask(tq, tq))

    def cond(c):
        return jnp.logical_and(c[0] < qi, c[1] < R_DONE)

    def body(c):
        return c[0] + 1, block(pl.multiple_of((qi - 1 - c[0]) * tq, tq), None)

    _, rmin = lax.while_loop(cond, body, (jnp.int32(0), rmin))

    @pl.when(rmin < R_DONE)
    def _():
        z = jnp.dot(qs, kmt_scr[...], preferred_element_type=F32)
        _sb_block(z, vm_scr[...], u128_ref[...], _meta_mask(2 * tq, MBLK), r_scr, acc_scr)

    o_ref[0] = _unstack_heads(acc_scr[...]).astype(BF16)


def _suffix_sum_matrix(w):
    j = lax.broadcasted_iota(jnp.int32, (w, w), 0)
    s = lax.broadcasted_iota(jnp.int32, (w, w), 1)
    return (j > s).astype(BF16)


def _attn_prompt(q, kt, vt):
    b, t, d = q.shape
    tq = KBLK
    const2 = lambda bi, hp, qi: (0, 0)
    kv_spec = pl.BlockSpec((1, PAIR, N_META + t), lambda bi, hp, qi: (bi, hp, 0))
    return pl.pallas_call(
        _attn_prompt_kernel,
        grid=(b, d // PAIR, t // tq),
        in_specs=[
            pl.BlockSpec((1, tq, PAIR), lambda bi, hp, qi: (bi, qi, hp)),
            kv_spec, kv_spec,
            pl.BlockSpec((KBLK, KBLK), const2),
            pl.BlockSpec((MBLK, MBLK), const2),
        ],
        out_specs=pl.BlockSpec((1, tq, PAIR), lambda bi, hp, qi: (bi, qi, hp)),
        out_shape=jax.ShapeDtypeStruct((b, t, d), BF16),
        scratch_shapes=[pltpu.VMEM((PAIR, t), BF16), pltpu.VMEM((t, PAIR), BF16),
                        pltpu.VMEM((PAIR, MBLK), BF16), pltpu.VMEM((MBLK, PAIR), BF16),
                        pltpu.VMEM((2 * tq, PAIR), F32), pltpu.VMEM((2 * tq, PAIR), F32)],
        compiler_params=pltpu.CompilerParams(
            dimension_semantics=("arbitrary", "arbitrary", "arbitrary"), vmem_limit_bytes=VMEM_LIMIT),
        name="attn_prompt",
    )(q, kt, vt, _suffix_sum_matrix(KBLK), _suffix_sum_matrix(MBLK))


def _attn_sample_kernel(q_ref, kn_ref, vn_ref, ck_ref, cv_ref, km_ref, vm_ref, u256_ref, u128_ref, o_ref,
                        r_scr, acc_scr, *, s_len, n_past):
    qs = _stack_heads(q_ref[0])
    r_scr[...] = jnp.zeros_like(r_scr)
    acc_scr[...] = jnp.zeros_like(acc_scr)
    u256 = u256_ref[...]
    u128 = u128_ref[...]
    nt = (((1,), (1,)), ((), ()))

    z = lax.dot_general(qs, kn_ref[0], nt, preferred_element_type=F32)
    rmin = _sb_block(z, vn_ref[0], u128, _causal_mask(s_len, MBLK), r_scr, acc_scr)

    def cond(c):
        return jnp.logical_and(c[0] < n_past, c[1] < R_DONE)

    def body(c):
        start = pl.multiple_of((n_past - 1 - c[0]) * KBLK, KBLK)
        z = jnp.dot(qs, ck_ref[0, :, pl.ds(start, KBLK)].astype(BF16), preferred_element_type=F32)
        v = cv_ref[0, :, pl.ds(start, KBLK)].T.astype(BF16)
        return c[0] + 1, _sb_block(z, v, u256, None, r_scr, acc_scr)

    _, rmin = lax.while_loop(cond, body, (jnp.int32(0), rmin))

    @pl.when(rmin < R_DONE)
    def _():
        z = lax.dot_general(qs, km_ref[...], nt, preferred_element_type=F32)
        _sb_block(z, vm_ref[...], u128, _meta_mask(2 * s_len, MBLK), r_scr, acc_scr)

    o_ref[0] = _unstack_heads(acc_scr[...]).astype(BF16)


def _attn_sample(q, k_new, v_new, cache_k, cache_v, k_meta, v_meta):
    b, s_len, d = q.shape
    p = cache_k.shape[2]
    blk = lambda bi, hp: (bi, 0, hp)
    const2 = lambda bi, hp: (0, 0)
    cache_spec = pl.BlockSpec((1, PAIR, p), lambda bi, hp: (bi, hp, 0))
    return pl.pallas_call(
        functools.partial(_attn_sample_kernel, s_len=s_len, n_past=p // KBLK),
        grid=(b, d // PAIR),
        in_specs=[
            pl.BlockSpec((1, s_len, PAIR), blk),
            pl.BlockSpec((1, MBLK, PAIR), blk), pl.BlockSpec((1, MBLK, PAIR), blk),
            cache_spec, cache_spec,
            pl.BlockSpec((MBLK, PAIR), lambda bi, hp: (0, hp)),
            pl.BlockSpec((MBLK, PAIR), lambda bi, hp: (0, hp)),
            pl.BlockSpec((KBLK, KBLK), const2),
            pl.BlockSpec((MBLK, MBLK), const2),
        ],
        out_specs=pl.BlockSpec((1, s_len, PAIR), blk),
        out_shape=jax.ShapeDtypeStruct((b, s_len, d), BF16),
        scratch_shapes=[pltpu.VMEM((2 * s_len, PAIR), F32), pltpu.VMEM((2 * s_len, PAIR), F32)],
        compiler_params=pltpu.CompilerParams(
            dimension_semantics=("arbitrary", "arbitrary"), vmem_limit_bytes=VMEM_LIMIT),
        name="attn_sample",
    )(q, k_new, v_new, cache_k, cache_v, k_meta, v_meta, _suffix_sum_matrix(KBLK), _suffix_sum_matrix(MBLK))


def _merge_mlp_kernel(x_ref, conv_ref, attn_ref, gates_ref, wout_ref, gmlp_ref, wup_ref, wdown_ref, gfin_ref,
                      y_ref):
    g = gates_ref[...]
    mixed = (g[:, :D_MODEL].astype(F32) * conv_ref[...].astype(F32)
             + g[:, D_MODEL:].astype(F32) * attn_ref[...].astype(F32))
    x1 = x_ref[...] + jnp.dot(mixed.astype(BF16), wout_ref[...], preferred_element_type=F32)
    h = _rms(x1, gmlp_ref[...]).astype(BF16)
    acc = x1
    for c in range(D_FF // D_MODEL):
        cs = slice(c * D_MODEL, (c + 1) * D_MODEL)
        u = jnp.maximum(jnp.dot(h, wup_ref[:, cs], preferred_element_type=F32), 0.0)
        acc = acc + jnp.dot((u * u).astype(BF16), wdown_ref[cs, :], preferred_element_type=F32)
    y_ref[...] = _rms(acc, gfin_ref[...])


def _merge_mlp(x, conv, attn, gates, w_out, g_mlp, w_up, w_down, g_final, tm):
    m, d = x.shape
    row = lambda i: (i, 0)
    const = lambda i: (0, 0)
    resident = functools.partial(pl.BlockSpec, index_map=const, pipeline_mode=pl.Buffered(1))
    return pl.pallas_call(
        _merge_mlp_kernel,
        grid=(m // tm,),
        in_specs=[
            pl.BlockSpec((tm, d), row), pl.BlockSpec((tm, d), row), pl.BlockSpec((tm, d), row),
            pl.BlockSpec((tm, 2 * d), row),
            resident((d, d)), resident((1, d)), resident((d, D_FF)), resident((D_FF, d)), resident((1, d)),
        ],
        out_specs=pl.BlockSpec((tm, d), row),
        out_shape=jax.ShapeDtypeStruct((m, d), F32),
        compiler_params=pltpu.CompilerParams(
            dimension_semantics=("arbitrary",), vmem_limit_bytes=VMEM_LIMIT),
        name="merge_mlp",
    )(x, conv, attn, gates, w_out, g_mlp.reshape(1, d), w_up, w_down, g_final.reshape(1, d))


def kernel(x_prompt, x_sample, cache_k, cache_v, cache_conv, meta, g_mix, w_in, w_dw, b_dw, g_ln_conv,
           b_ln_conv, w_pw2, w_out, g_mlp, w_up, w_down, g_final):
    b, t, d = x_prompt.shape
    sb, s_len, _ = x_sample.shape
    depth, _, past, _, _ = cache_k.shape
    assert depth == 1 and d == D_MODEL and meta.shape == (N_META, d)
    assert t % 512 == 0 and past % KBLK == 0 and s_len % 16 == 0 and CONV_K - 1 <= s_len <= MBLK
    n_s = sb * s_len

    w_in_b = w_in[0].astype(BF16)
    w_pw2_b = w_pw2[0].astype(BF16)
    w_out_b = w_out[0].astype(BF16)
    w_up_b = w_up[0].astype(BF16)
    w_down_b = w_down[0].astype(BF16)

    w_main_b = jnp.concatenate([w_in_b[:, :3 * d], w_in_b[:, 5 * d:]], axis=1)
    w_kvt_b = w_in_b[:, 3 * d:5 * d].T
    glu_p3, q_p, gates_p, kt_p, vt_p = _in_proj_prompt(x_prompt, meta.astype(F32), g_mix[0], w_main_b, w_kvt_b,
                                                       tm=512)
    x_sm = jnp.concatenate([x_sample.reshape(n_s, d), meta.astype(F32)], axis=0)
    glu_s, q_s, k_s, kb_s, v_s, vb_s, gates_s = _in_proj(x_sm, g_mix[0], w_in_b, tm=n_s + N_META)

    pad_meta = ((0, MBLK - N_META), (0, 0))
    kb_meta = jnp.pad(kb_s[n_s:], pad_meta)
    vb_meta = jnp.pad(vb_s[n_s:], pad_meta)

    conv_w = (w_dw[0], b_dw[0], g_ln_conv[0], b_ln_conv[0], w_pw2_b)
    head_p = jnp.pad(glu_s[n_s:], ((HALO - N_META, 0), (0, 0)))[None]
    conv_p = _conv_branch(glu_p3, head_p, *conv_w, tm=256)
    glu_s3 = glu_s[:n_s].reshape(sb, s_len, d)
    head_s = jnp.pad(cache_conv[0], ((0, 0), (HALO - (CONV_K - 1), 0), (0, 0)))
    conv_s = _conv_branch(glu_s3, head_s, *conv_w, tm=s_len)

    attn_p = _attn_prompt(q_p, kt_p, vt_p)
    pad_new = ((0, 0), (0, MBLK - s_len), (0, 0))
    feature_major = lambda c: c.transpose(0, 2, 3, 1).reshape(sb, d, past)
    attn_s = _attn_sample(
        q_s[:n_s].reshape(sb, s_len, d),
        jnp.pad(kb_s[:n_s].reshape(sb, s_len, d), pad_new), jnp.pad(vb_s[:n_s].reshape(sb, s_len, d), pad_new),
        feature_major(cache_k[0]), feature_major(cache_v[0]), kb_meta, vb_meta)

    mlp_w = (w_out_b, g_mlp[0], w_up_b, w_down_b, g_final)
    y_p = _merge_mlp(x_prompt.reshape(b * t, d), conv_p.reshape(b * t, d), attn_p.reshape(b * t, d),
                     gates_p.reshape(b * t, 2 * d), *mlp_w, tm=512)
    y_s = _merge_mlp(x_sample.reshape(n_s, d), conv_s.reshape(n_s, d), attn_s.reshape(n_s, d), gates_s[:n_s],
                     *mlp_w, tm=n_s)

    def key_major(xt):
        return xt.reshape(1, b, N_HEADS, HEAD_DIM, N_META + t).transpose(0, 1, 4, 2, 3)

    return (
        y_p.reshape(b, t, d),
        y_s.reshape(sb, s_len, d),
        key_major(kt_p),
        key_major(vt_p),
        glu_p3[:, t - (CONV_K - 1):][None],
        k_s[:n_s].reshape(1, sb, s_len, N_HEADS, HEAD_DIM),
        v_s[:n_s].reshape(1, sb, s_len, N_HEADS, HEAD_DIM),
        glu_s3[:, s_len - (CONV_K - 1):][None],
    )
```

```python
import functools

import jax
import jax.numpy as jnp
from jax import lax
from jax.experimental import pallas as pl
from jax.experimental.pallas import tpu as pltpu

F32 = jnp.float32
BF16 = jnp.bfloat16

D_MODEL = 1024
N_META = 16
CONV_K = 31
N_HEADS = 16
HEAD_DIM = 64
D_FF = 4 * D_MODEL
EPS = 1e-6
ATTN_SCALE = HEAD_DIM ** -0.5
LOG2E = 1.4426950408889634
PAIR = 2 * HEAD_DIM
SUBLANES = 8
HALO = 32
KBLK = 256
MBLK = 128
Q_SUB = 2
MASKED = -1e30
R_DONE = 160.0
VMEM_LIMIT = 56 * 1024 * 1024


def _sigmoid(x):
    return 1.0 / (1.0 + jnp.exp(-x))


def _rms(x, g):
    return x * lax.rsqrt(jnp.mean(x * x, axis=-1, keepdims=True) + EPS) * g


def _in_proj_kernel(x_ref, g_ref, w_ref, wb_ref, glu_ref, q_ref, k_ref, kb_ref, v_ref, vb_ref,
                    gates_ref, h_scr):
    j = pl.program_id(1)

    @pl.when(j == 0)
    def _():
        h_scr[...] = _rms(x_ref[...], g_ref[...]).astype(BF16)

    h = h_scr[...]
    p = jnp.dot(h, w_ref[...], preferred_element_type=F32)

    @pl.when(j == 0)
    def _():
        b = jnp.dot(h, wb_ref[...], preferred_element_type=F32)
        glu_ref[...] = p * _sigmoid(b)

    @pl.when(j == 1)
    def _():
        q_ref[...] = (p * (ATTN_SCALE * LOG2E)).astype(BF16)

    @pl.when(j == 2)
    def _():
        k_ref[...] = p
        kb_ref[...] = p.astype(BF16)

    @pl.when(j == 3)
    def _():
        v_ref[...] = p
        vb_ref[...] = p.astype(BF16)

    @pl.when(j == 4)
    def _():
        gates_ref[:, :D_MODEL] = _sigmoid(p).astype(BF16)

    @pl.when(j == 5)
    def _():
        gates_ref[:, D_MODEL:] = _sigmoid(p).astype(BF16)


def _in_proj(x, g, w_in, tm):
    m = x.shape[0]
    d = D_MODEL
    row = lambda i, j: (i, 0)
    return pl.pallas_call(
        _in_proj_kernel,
        grid=(m // tm, 6),
        in_specs=[
            pl.BlockSpec((tm, d), row),
            pl.BlockSpec((1, d), lambda i, j: (0, 0)),
            pl.BlockSpec((d, d), lambda i, j: (0, jnp.where(j == 0, 0, j + 1))),
            pl.BlockSpec((d, d), lambda i, j: (0, 1)),
        ],
        out_specs=[
            pl.BlockSpec((tm, d), row), pl.BlockSpec((tm, d), row),
            pl.BlockSpec((tm, d), row), pl.BlockSpec((tm, d), row),
            pl.BlockSpec((tm, d), row), pl.BlockSpec((tm, d), row),
            pl.BlockSpec((tm, 2 * d), row),
        ],
        out_shape=[
            jax.ShapeDtypeStruct((m, d), F32), jax.ShapeDtypeStruct((m, d), BF16),
            jax.ShapeDtypeStruct((m, d), F32), jax.ShapeDtypeStruct((m, d), BF16),
            jax.ShapeDtypeStruct((m, d), F32), jax.ShapeDtypeStruct((m, d), BF16),
            jax.ShapeDtypeStruct((m, 2 * d), BF16),
        ],
        scratch_shapes=[pltpu.VMEM((tm, d), BF16)],
        compiler_params=pltpu.CompilerParams(
            dimension_semantics=("arbitrary", "arbitrary"), vmem_limit_bytes=VMEM_LIMIT),
        name="in_proj",
    )(x, g.reshape(1, d), w_in, w_in)


def _in_proj_prompt_kernel(x_ref, meta_ref, g_ref, w_ref, wt_ref, glu_ref, q_ref, gates_ref, kt_ref, vt_ref,
                           h_scr, hs_scr, carry_scr, *, tm, n_tiles):
    t = pl.program_id(1)
    d = D_MODEL
    nt = (((1,), (1,)), ((), ()))

    @pl.when(t == 0)
    def _():
        carry_scr[...] = _rms(meta_ref[...], g_ref[...]).astype(BF16)

    @pl.when(t < n_tiles)
    def _():
        h_scr[...] = _rms(x_ref[0], g_ref[...]).astype(BF16)

    hs_scr[0:N_META, :] = carry_scr[...]
    hs_scr[N_META:tm, :] = h_scr[0:tm - N_META, :]
    carry_scr[...] = h_scr[tm - N_META:tm, :]

    @pl.when(t < n_tiles)
    def _():
        h = h_scr[...]
        proj = lambda c: jnp.dot(h, w_ref[:, c * d:(c + 1) * d], preferred_element_type=F32)
        glu_ref[0] = proj(0) * _sigmoid(proj(1))
        q_ref[0] = (proj(2) * (ATTN_SCALE * LOG2E)).astype(BF16)
        gates_ref[0, :, :d] = _sigmoid(proj(3)).astype(BF16)
        gates_ref[0, :, d:] = _sigmoid(proj(4)).astype(BF16)

    hs = hs_scr[...]
    kt_ref[0] = lax.dot_general(wt_ref[0:d, :], hs, nt, preferred_element_type=F32)
    vt_ref[0] = lax.dot_general(wt_ref[d:2 * d, :], hs, nt, preferred_element_type=F32)


def _in_proj_prompt(x, meta, g, w_main, w_kvt, tm):
    b, t_len, d = x.shape
    n_tiles = t_len // tm
    row = lambda bi, t: (bi, jnp.minimum(t, n_tiles - 1), 0)
    const = lambda bi, t: (0, 0)
    resident = functools.partial(pl.BlockSpec, index_map=const, pipeline_mode=pl.Buffered(1))
    return pl.pallas_call(
        functools.partial(_in_proj_prompt_kernel, tm=tm, n_tiles=n_tiles),
        grid=(b, n_tiles + 1),
        in_specs=[
            pl.BlockSpec((1, tm, d), row),
            resident((N_META, d)), resident((1, d)), resident((d, 5 * d)), resident((2 * d, d)),
        ],
        out_specs=[
            pl.BlockSpec((1, tm, d), row), pl.BlockSpec((1, tm, d), row), pl.BlockSpec((1, tm, 2 * d), row),
            pl.BlockSpec((1, d, tm), lambda bi, t: (bi, 0, t)),
            pl.BlockSpec((1, d, tm), lambda bi, t: (bi, 0, t)),
        ],
        out_shape=[
            jax.ShapeDtypeStruct((b, t_len, d), F32), jax.ShapeDtypeStruct((b, t_len, d), BF16),
            jax.ShapeDtypeStruct((b, t_len, 2 * d), BF16),
            jax.ShapeDtypeStruct((b, d, N_META + t_len), F32), jax.ShapeDtypeStruct((b, d, N_META + t_len), F32),
        ],
        scratch_shapes=[pltpu.VMEM((tm, d), BF16), pltpu.VMEM((tm, d), BF16), pltpu.VMEM((N_META, d), BF16)],
        compiler_params=pltpu.CompilerParams(
            dimension_semantics=("arbitrary", "arbitrary"), vmem_limit_bytes=VMEM_LIMIT),
        name="in_proj_prompt",
    )(x, meta, g.reshape(1, d), w_main, w_kvt)


def _conv_kernel(glu_ref, prev_ref, head_ref, wdw_ref, bdw_ref, gln_ref, bln_ref, wpw_ref, out_ref,
                 ext_scr, y_scr, *, tm):
    i = pl.program_id(1)
    ext_scr[0, 0:HALO, :] = jnp.where(i == 0, head_ref[0], prev_ref[0])
    ext_scr[0, HALO:HALO + tm, :] = glu_ref[0]
    shifted_rows = HALO + tm - SUBLANES
    for m in range(1, SUBLANES):
        ext_scr[m, 0:shifted_rows, :] = ext_scr[0, m:m + shifted_rows, :]
    rc = min(tm, 64)
    first = HALO - (CONV_K - 1)
    for c in range(D_MODEL // 128):
        cs = slice(c * 128, (c + 1) * 128)
        for r in range(tm // rc):
            acc = jnp.broadcast_to(bdw_ref[:, cs], (rc, 128))
            for t in range(CONV_K):
                m = (t + first) % SUBLANES
                lo = r * rc + (t + first) - m
                acc = acc + wdw_ref[t:t + 1, cs] * ext_scr[m, lo:lo + rc, cs]
            y_scr[r * rc:(r + 1) * rc, cs] = acc
    y = y_scr[...]
    yc = y - jnp.mean(y, axis=-1, keepdims=True)
    var = jnp.mean(yc * yc, axis=-1, keepdims=True)
    ln = yc * lax.rsqrt(var + EPS) * gln_ref[...] + bln_ref[...]
    s = ln * _sigmoid(ln)
    out_ref[0] = jnp.dot(s.astype(BF16), wpw_ref[...], preferred_element_type=F32).astype(BF16)


def _conv_branch(glu, head, w_dw, b_dw, g_ln, b_ln, w_pw2, tm):
    b, t, d = glu.shape
    per_batch_head = head.shape[0] != 1
    const = lambda bi, i: (0, 0)
    return pl.pallas_call(
        functools.partial(_conv_kernel, tm=tm),
        grid=(b, t // tm),
        in_specs=[
            pl.BlockSpec((1, tm, d), lambda bi, i: (bi, i, 0)),
            pl.BlockSpec((1, HALO, d), lambda bi, i: (bi, jnp.maximum(i * (tm // HALO) - 1, 0), 0)),
            pl.BlockSpec((1, HALO, d), (lambda bi, i: (bi, 0, 0)) if per_batch_head else (lambda bi, i: (0, 0, 0))),
            pl.BlockSpec((CONV_K, d), const),
            pl.BlockSpec((1, d), const), pl.BlockSpec((1, d), const), pl.BlockSpec((1, d), const),
            pl.BlockSpec((d, d), const),
        ],
        out_specs=pl.BlockSpec((1, tm, d), lambda bi, i: (bi, i, 0)),
        out_shape=jax.ShapeDtypeStruct((b, t, d), BF16),
        scratch_shapes=[pltpu.VMEM((SUBLANES, HALO + tm, d), F32), pltpu.VMEM((tm, d), F32)],
        compiler_params=pltpu.CompilerParams(
            dimension_semantics=("arbitrary", "arbitrary"), vmem_limit_bytes=VMEM_LIMIT),
        name="conv_branch",
    )(glu, glu, head, w_dw, b_dw.reshape(1, d), g_ln.reshape(1, d), b_ln.reshape(1, d), w_pw2)


def _stack_heads(q):
    lane = lax.broadcasted_iota(jnp.int32, q.shape, 1)
    zero = jnp.zeros_like(q)
    return jnp.concatenate([jnp.where(lane < HEAD_DIM, q, zero), jnp.where(lane >= HEAD_DIM, q, zero)], axis=0)


def _unstack_heads(acc):
    r = acc.shape[0] // 2
    lane = lax.broadcasted_iota(jnp.int32, (r, PAIR), 1)
    return jnp.where(lane < HEAD_DIM, acc[:r], acc[r:])


def _sb_scores(z, mask):
    neg_abs = lax.bitcast_convert_type(lax.bitcast_convert_type(z, jnp.uint32) | jnp.uint32(0x80000000), F32)
    sp = jnp.maximum(z, 0.0) + jnp.log2(1.0 + jnp.exp2(neg_abs))
    ls = z - sp
    if mask is not None:
        sp = jnp.where(mask, sp, 0.0)
        ls = jnp.where(mask, ls, MASKED)
    return ls, sp.astype(BF16)


def _sb_weights(ls, spb, u, r):
    cs = jnp.dot(spb, u, preferred_element_type=F32)
    x = ls - cs
    total = cs[:, 0:1] + spb[:, 0:1].astype(F32)
    mass = jnp.broadcast_to(total, (ls.shape[0], PAIR))
    if r is not None:
        x = x - (r if ls.shape[1] == PAIR else jnp.concatenate([r] * (ls.shape[1] // PAIR), axis=1))
        mass = mass + r
    return jnp.exp2(x).astype(BF16), mass


def _sb_first(z_d, v_d, mask_d, u_d, z_p=None, v_p=None, u_p=None):
    ls_d, spb_d = _sb_scores(z_d, mask_d)
    if z_p is not None:
        ls_p, spb_p = _sb_scores(z_p, None)
    a_d, r = _sb_weights(ls_d, spb_d, u_d, None)
    acc = jnp.dot(a_d, v_d, preferred_element_type=F32)
    if z_p is not None:
        a_p, r = _sb_weights(ls_p, spb_p, u_p, r)
        acc = acc + jnp.dot(a_p, v_p, preferred_element_type=F32)
    return r, acc


def _sb_block(z, v, u, mask, r_ref, acc_ref):
    ls, spb = _sb_scores(z, mask)
    a, r = _sb_weights(ls, spb, u, r_ref[...])
    acc_ref[...] += jnp.dot(a, v, preferred_element_type=F32)
    r_ref[...] = r
    return jnp.min(r)


def _causal_mask(rows_per_head, width):
    row = lax.broadcasted_iota(jnp.int32, (2 * rows_per_head, width), 0)
    col = lax.broadcasted_iota(jnp.int32, (2 * rows_per_head, width), 1)
    return col < jnp.where(row >= rows_per_head, row - rows_per_head, row)


def _meta_mask(rows, width):
    return lax.broadcasted_iota(jnp.int32, (rows, width), 1) < N_META


def _attn_prompt_kernel(q_ref, kt_ref, vt_ref, u256_ref, u128_ref, o_ref,
                        kt_scr, v_scr, kmt_scr, vm_scr, r_scr, acc_scr):
    g = pl.program_id(2)
    tq = KBLK

    @pl.when(g == 0)
    def _():
        for c in range(v_scr.shape[0] // KBLK):
            cols = slice(N_META + c * KBLK, N_META + (c + 1) * KBLK)
            kt_scr[:, c * KBLK:(c + 1) * KBLK] = kt_ref[0, :, cols].astype(BF16)
            v_scr[c * KBLK:(c + 1) * KBLK, :] = vt_ref[0, :, cols].T.astype(BF16)
        kmt_scr[...] = kt_ref[0, :, 0:MBLK].astype(BF16)
        vm_scr[...] = vt_ref[0, :, 0:MBLK].T.astype(BF16)

    u256 = u256_ref[...]
    n_sub = q_ref.shape[1] // tq
    qs = [_stack_heads(q_ref[0, s * tq:(s + 1) * tq, :]) for s in range(n_sub)]

    def scores(s, j):
        return jnp.dot(qs[s], kt_scr[:, pl.ds(pl.multiple_of(j * tq, tq), tq)], preferred_element_type=F32)

    def values(j):
        return v_scr[pl.ds(pl.multiple_of(j * tq, tq), tq), :]

    def first_blocks(first_has_past):
        mask = _causal_mask(tq, tq)
        for s in range(n_sub):
            qi = g * n_sub + s
            if s > 0 or first_has_past:
                r, acc = _sb_first(scores(s, qi), values(qi), mask, u256, scores(s, qi - 1), values(qi - 1), u256)
            else:
                r, acc = _sb_first(scores(s, qi), values(qi), mask, u256)
            r_scr[s] = r
            acc_scr[s] = acc

    @pl.when(g == 0)
    def _():
        first_blocks(False)

    @pl.when(g > 0)
    def _():
        first_blocks(True)

    for s in range(n_sub):
        qi = g * n_sub + s
        r_ref, acc_ref = r_scr.at[s], acc_scr.at[s]

        def cond(c, qi=qi):
            return jnp.logical_and(c[0] < qi - 1, c[1] < R_DONE)

        def body(c, s=s, qi=qi, r_ref=r_ref, acc_ref=acc_ref):
            j = qi - 2 - c[0]
            return c[0] + 1, _sb_block(scores(s, j), values(j), u256, None, r_ref, acc_ref)

        _, rmin = lax.while_loop(cond, body, (jnp.int32(0), jnp.min(r_ref[...])))

        @pl.when(rmin < R_DONE)
        def _(s=s, r_ref=r_ref, acc_ref=acc_ref):
            z = jnp.dot(qs[s], kmt_scr[...], preferred_element_type=F32)
            _sb_block(z, vm_scr[...], u128_ref[...], _meta_mask(2 * tq, MBLK), r_ref, acc_ref)

        o_ref[0, s * tq:(s + 1) * tq, :] = _unstack_heads(acc_ref[...]).astype(BF16)


def _suffix_sum_matrix(w):
    j = lax.broadcasted_iota(jnp.int32, (w, w), 0)
    s = lax.broadcasted_iota(jnp.int32, (w, w), 1)
    return (j > s).astype(BF16)


def _attn_prompt(q, kt, vt):
    b, t, d = q.shape
    tq = Q_SUB * KBLK
    const2 = lambda bi, hp, g: (0, 0)
    kv_spec = pl.BlockSpec((1, PAIR, N_META + t), lambda bi, hp, g: (bi, hp, 0))
    return pl.pallas_call(
        _attn_prompt_kernel,
        grid=(b, d // PAIR, t // tq),
        in_specs=[
            pl.BlockSpec((1, tq, PAIR), lambda bi, hp, g: (bi, g, hp)),
            kv_spec, kv_spec,
            pl.BlockSpec((KBLK, KBLK), const2),
            pl.BlockSpec((MBLK, MBLK), const2),
        ],
        out_specs=pl.BlockSpec((1, tq, PAIR), lambda bi, hp, g: (bi, g, hp)),
        out_shape=jax.ShapeDtypeStruct((b, t, d), BF16),
        scratch_shapes=[pltpu.VMEM((PAIR, t), BF16), pltpu.VMEM((t, PAIR), BF16),
                        pltpu.VMEM((PAIR, MBLK), BF16), pltpu.VMEM((MBLK, PAIR), BF16),
                        pltpu.VMEM((Q_SUB, 2 * KBLK, PAIR), F32), pltpu.VMEM((Q_SUB, 2 * KBLK, PAIR), F32)],
        compiler_params=pltpu.CompilerParams(
            dimension_semantics=("arbitrary", "arbitrary", "arbitrary"), vmem_limit_bytes=VMEM_LIMIT),
        name="attn_prompt",
    )(q, kt, vt, _suffix_sum_matrix(KBLK), _suffix_sum_matrix(MBLK))


def _attn_sample_kernel(q_ref, kn_ref, vn_ref, ck_ref, cv_ref, km_ref, vm_ref, u256_ref, u128_ref, o_ref,
                        r_scr, acc_scr, *, s_len, n_past):
    qs = _stack_heads(q_ref[0])
    u256 = u256_ref[...]
    u128 = u128_ref[...]
    nt = (((1,), (1,)), ((), ()))

    def cache_block(j):
        start = pl.multiple_of(j * KBLK, KBLK)
        z = jnp.dot(qs, ck_ref[0, :, pl.ds(start, KBLK)].astype(BF16), preferred_element_type=F32)
        return z, cv_ref[0, :, pl.ds(start, KBLK)].T.astype(BF16)

    z_new = lax.dot_general(qs, kn_ref[0], nt, preferred_element_type=F32)
    z_past, v_past = cache_block(n_past - 1)
    r, acc = _sb_first(z_new, vn_ref[0], _causal_mask(s_len, MBLK), u128, z_past, v_past, u256)
    r_scr[...] = r
    acc_scr[...] = acc

    def cond(c):
        return jnp.logical_and(c[0] < n_past - 1, c[1] < R_DONE)

    def body(c):
        z, v = cache_block(n_past - 2 - c[0])
        return c[0] + 1, _sb_block(z, v, u256, None, r_scr, acc_scr)

    _, rmin = lax.while_loop(cond, body, (jnp.int32(0), jnp.min(r)))

    @pl.when(rmin < R_DONE)
    def _():
        z = lax.dot_general(qs, km_ref[...], nt, preferred_element_type=F32)
        _sb_block(z, vm_ref[...], u128, _meta_mask(2 * s_len, MBLK), r_scr, acc_scr)

    o_ref[0] = _unstack_heads(acc_scr[...]).astype(BF16)


def _attn_sample(q, k_new, v_new, cache_k, cache_v, k_meta, v_meta):
    b, s_len, d = q.shape
    p = cache_k.shape[2]
    blk = lambda bi, hp: (bi, 0, hp)
    const2 = lambda bi, hp: (0, 0)
    cache_spec = pl.BlockSpec((1, PAIR, p), lambda bi, hp: (bi, hp, 0))
    return pl.pallas_call(
        functools.partial(_attn_sample_kernel, s_len=s_len, n_past=p // KBLK),
        grid=(b, d // PAIR),
        in_specs=[
            pl.BlockSpec((1, s_len, PAIR), blk),
            pl.BlockSpec((1, MBLK, PAIR), blk), pl.BlockSpec((1, MBLK, PAIR), blk),
            cache_spec, cache_spec,
            pl.BlockSpec((MBLK, PAIR), lambda bi, hp: (0, hp)),
            pl.BlockSpec((MBLK, PAIR), lambda bi, hp: (0, hp)),
            pl.BlockSpec((KBLK, KBLK), const2),
            pl.BlockSpec((MBLK, MBLK), const2),
        ],
        out_specs=pl.BlockSpec((1, s_len, PAIR), blk),
        out_shape=jax.ShapeDtypeStruct((b, s_len, d), BF16),
        scratch_shapes=[pltpu.VMEM((2 * s_len, PAIR), F32), pltpu.VMEM((2 * s_len, PAIR), F32)],
        compiler_params=pltpu.CompilerParams(
            dimension_semantics=("arbitrary", "arbitrary"), vmem_limit_bytes=VMEM_LIMIT),
        name="attn_sample",
    )(q, k_new, v_new, cache_k, cache_v, k_meta, v_meta, _suffix_sum_matrix(KBLK), _suffix_sum_matrix(MBLK))


def _merge_mlp_kernel(x_ref, conv_ref, attn_ref, gates_ref, wout_ref, gmlp_ref, wup_ref, wdown_ref, gfin_ref,
                      y_ref):
    g = gates_ref[...]
    mixed = (g[:, :D_MODEL].astype(F32) * conv_ref[...].astype(F32)
             + g[:, D_MODEL:].astype(F32) * attn_ref[...].astype(F32))
    x1 = x_ref[...] + jnp.dot(mixed.astype(BF16), wout_ref[...], preferred_element_type=F32)
    h = _rms(x1, gmlp_ref[...]).astype(BF16)
    acc = x1
    for c in range(D_FF // D_MODEL):
        cs = slice(c * D_MODEL, (c + 1) * D_MODEL)
        u = jnp.maximum(jnp.dot(h, wup_ref[:, cs], preferred_element_type=F32), 0.0)
        acc = acc + jnp.dot((u * u).astype(BF16), wdown_ref[cs, :], preferred_element_type=F32)
    y_ref[...] = _rms(acc, gfin_ref[...])


def _merge_mlp(x, conv, attn, gates, w_out, g_mlp, w_up, w_down, g_final, tm):
    m, d = x.shape
    row = lambda i: (i, 0)
    const = lambda i: (0, 0)
    resident = functools.partial(pl.BlockSpec, index_map=const, pipeline_mode=pl.Buffered(1))
    return pl.pallas_call(
        _merge_mlp_kernel,
        grid=(m // tm,),
        in_specs=[
            pl.BlockSpec((tm, d), row), pl.BlockSpec((tm, d), row), pl.BlockSpec((tm, d), row),
            pl.BlockSpec((tm, 2 * d), row),
            resident((d, d)), resident((1, d)), resident((d, D_FF)), resident((D_FF, d)), resident((1, d)),
        ],
        out_specs=pl.BlockSpec((tm, d), row),
        out_shape=jax.ShapeDtypeStruct((m, d), F32),
        compiler_params=pltpu.CompilerParams(
            dimension_semantics=("arbitrary",), vmem_limit_bytes=VMEM_LIMIT),
        name="merge_mlp",
    )(x, conv, attn, gates, w_out, g_mlp.reshape(1, d), w_up, w_down, g_final.reshape(1, d))


def kernel(x_prompt, x_sample, cache_k, cache_v, cache_conv, meta, g_mix, w_in, w_dw, b_dw, g_ln_conv,
           b_ln_conv, w_pw2, w_out, g_mlp, w_up, w_down, g_final):
    b, t, d = x_prompt.shape
    sb, s_len, _ = x_sample.shape
    depth, _, past, _, _ = cache_k.shape
    assert depth == 1 and d == D_MODEL and meta.shape == (N_META, d)
    assert t % 512 == 0 and past % KBLK == 0 and s_len % 16 == 0 and CONV_K - 1 <= s_len <= MBLK
    n_s = sb * s_len

    w_in_b = w_in[0].astype(BF16)
    w_pw2_b = w_pw2[0].astype(BF16)
    w_out_b = w_out[0].astype(BF16)
    w_up_b = w_up[0].astype(BF16)
    w_down_b = w_down[0].astype(BF16)

    w_main_b = jnp.concatenate([w_in_b[:, :3 * d], w_in_b[:, 5 * d:]], axis=1)
    w_kvt_b = w_in_b[:, 3 * d:5 * d].T
    glu_p3, q_p, gates_p, kt_p, vt_p = _in_proj_prompt(x_prompt, meta.astype(F32), g_mix[0], w_main_b, w_kvt_b,
                                                       tm=512)
    x_sm = jnp.concatenate([x_sample.reshape(n_s, d), meta.astype(F32)], axis=0)
    glu_s, q_s, k_s, kb_s, v_s, vb_s, gates_s = _in_proj(x_sm, g_mix[0], w_in_b, tm=n_s + N_META)

    pad_meta = ((0, MBLK - N_META), (0, 0))
    kb_meta = jnp.pad(kb_s[n_s:], pad_meta)
    vb_meta = jnp.pad(vb_s[n_s:], pad_meta)

    conv_w = (w_dw[0], b_dw[0], g_ln_conv[0], b_ln_conv[0], w_pw2_b)
    head_p = jnp.pad(glu_s[n_s:], ((HALO - N_META, 0), (0, 0)))[None]
    conv_p = _conv_branch(glu_p3, head_p, *conv_w, tm=256)
    glu_s3 = glu_s[:n_s].reshape(sb, s_len, d)
    head_s = jnp.pad(cache_conv[0], ((0, 0), (HALO - (CONV_K - 1), 0), (0, 0)))
    conv_s = _conv_branch(glu_s3, head_s, *conv_w, tm=s_len)

    attn_p = _attn_prompt(q_p, kt_p, vt_p)
    pad_new = ((0, 0), (0, MBLK - s_len), (0, 0))
    feature_major = lambda c: c.transpose(0, 2, 3, 1).reshape(sb, d, past)
    attn_s = _attn_sample(
        q_s[:n_s].reshape(sb, s_len, d),
        jnp.pad(kb_s[:n_s].reshape(sb, s_len, d), pad_new), jnp.pad(vb_s[:n_s].reshape(sb, s_len, d), pad_new),
        feature_major(cache_k[0]), feature_major(cache_v[0]), kb_meta, vb_meta)

    mlp_w = (w_out_b, g_mlp[0], w_up_b, w_down_b, g_final)
    y_p = _merge_mlp(x_prompt.reshape(b * t, d), conv_p.reshape(b * t, d), attn_p.reshape(b * t, d),
                     gates_p.reshape(b * t, 2 * d), *mlp_w, tm=512)
    y_s = _merge_mlp(x_sample.reshape(n_s, d), conv_s.reshape(n_s, d), attn_s.reshape(n_s, d), gates_s[:n_s],
                     *mlp_w, tm=n_s)

    def key_major(xt):
        return xt.reshape(1, b, N_HEADS, HEAD_DIM, N_META + t).transpose(0, 1, 4, 2, 3)

    return (
        y_p.reshape(b, t, d),
        y_s.reshape(sb, s_len, d),
        key_major(kt_p),
        key_major(vt_p),
        glu_p3[:, t - (CONV_K - 1):][None],
        k_s[:n_s].reshape(1, sb, s_len, N_HEADS, HEAD_DIM),
        v_s[:n_s].reshape(1, sb, s_len, N_HEADS, HEAD_DIM),
        glu_s3[:, s_len - (CONV_K - 1):][None],
    )
```

```python
import functools

import jax
import jax.numpy as jnp
from jax import lax
from jax.experimental import pallas as pl
from jax.experimental.pallas import tpu as pltpu

F32 = jnp.float32
BF16 = jnp.bfloat16

D_MODEL = 1024
N_META = 16
CONV_K = 31
N_HEADS = 16
HEAD_DIM = 64
D_FF = 4 * D_MODEL
EPS = 1e-6
ATTN_SCALE = HEAD_DIM ** -0.5
LOG2E = 1.4426950408889634
PAIR = 2 * HEAD_DIM
SUBLANES = 8
HALO = 32
KBLK = 256
MBLK = 128
Q_SUB = 4
MASKED = -1e30
R_DONE = 160.0
VMEM_LIMIT = 56 * 1024 * 1024


def _sigmoid(x):
    return 1.0 / (1.0 + jnp.exp(-x))


def _rms(x, g):
    return x * lax.rsqrt(jnp.mean(x * x, axis=-1, keepdims=True) + EPS) * g


def _in_proj_kernel(x_ref, g_ref, w_ref, wb_ref, glu_ref, q_ref, k_ref, kb_ref, v_ref, vb_ref,
                    gates_ref, h_scr):
    j = pl.program_id(1)

    @pl.when(j == 0)
    def _():
        h_scr[...] = _rms(x_ref[...], g_ref[...]).astype(BF16)

    h = h_scr[...]
    p = jnp.dot(h, w_ref[...], preferred_element_type=F32)

    @pl.when(j == 0)
    def _():
        b = jnp.dot(h, wb_ref[...], preferred_element_type=F32)
        glu_ref[...] = p * _sigmoid(b)

    @pl.when(j == 1)
    def _():
        q_ref[...] = (p * (ATTN_SCALE * LOG2E)).astype(BF16)

    @pl.when(j == 2)
    def _():
        k_ref[...] = p
        kb_ref[...] = p.astype(BF16)

    @pl.when(j == 3)
    def _():
        v_ref[...] = p
        vb_ref[...] = p.astype(BF16)

    @pl.when(j == 4)
    def _():
        gates_ref[:, :D_MODEL] = _sigmoid(p).astype(BF16)

    @pl.when(j == 5)
    def _():
        gates_ref[:, D_MODEL:] = _sigmoid(p).astype(BF16)


def _in_proj(x, g, w_in, tm):
    m = x.shape[0]
    d = D_MODEL
    row = lambda i, j: (i, 0)
    return pl.pallas_call(
        _in_proj_kernel,
        grid=(m // tm, 6),
        in_specs=[
            pl.BlockSpec((tm, d), row),
            pl.BlockSpec((1, d), lambda i, j: (0, 0)),
            pl.BlockSpec((d, d), lambda i, j: (0, jnp.where(j == 0, 0, j + 1))),
            pl.BlockSpec((d, d), lambda i, j: (0, 1)),
        ],
        out_specs=[
            pl.BlockSpec((tm, d), row), pl.BlockSpec((tm, d), row),
            pl.BlockSpec((tm, d), row), pl.BlockSpec((tm, d), row),
            pl.BlockSpec((tm, d), row), pl.BlockSpec((tm, d), row),
            pl.BlockSpec((tm, 2 * d), row),
        ],
        out_shape=[
            jax.ShapeDtypeStruct((m, d), F32), jax.ShapeDtypeStruct((m, d), BF16),
            jax.ShapeDtypeStruct((m, d), F32), jax.ShapeDtypeStruct((m, d), BF16),
            jax.ShapeDtypeStruct((m, d), F32), jax.ShapeDtypeStruct((m, d), BF16),
            jax.ShapeDtypeStruct((m, 2 * d), BF16),
        ],
        scratch_shapes=[pltpu.VMEM((tm, d), BF16)],
        compiler_params=pltpu.CompilerParams(
            dimension_semantics=("arbitrary", "arbitrary"), vmem_limit_bytes=VMEM_LIMIT),
        name="in_proj",
    )(x, g.reshape(1, d), w_in, w_in)


def _depthwise_conv_block(c, ext_scr, sh_scr, y_scr, wdw_ref, bdw_ref, tm):
    rows = HALO + tm - SUBLANES
    rc = min(tm, 64)
    first = HALO - (CONV_K - 1)
    cs = slice(c * 128, (c + 1) * 128)
    sh = sh_scr.at[c % 2]
    for m in range(1, SUBLANES):
        sh[m - 1, 0:rows, :] = ext_scr[m:m + rows, cs]
    for r in range(tm // rc):
        acc = jnp.broadcast_to(bdw_ref[:, cs], (rc, 128))
        for t in range(CONV_K):
            m = (t + first) % SUBLANES
            lo = r * rc + (t + first) - m
            src = ext_scr[lo:lo + rc, cs] if m == 0 else sh[m - 1, lo:lo + rc, :]
            acc = acc + wdw_ref[t:t + 1, cs] * src
        y_scr[r * rc:(r + 1) * rc, cs] = acc


N_CONV_BLOCKS = D_MODEL // 128


def _ln_swish_pointwise(y, gln_ref, bln_ref, wpw_ref):
    yc = y - jnp.mean(y, axis=-1, keepdims=True)
    var = jnp.mean(yc * yc, axis=-1, keepdims=True)
    ln = yc * lax.rsqrt(var + EPS) * gln_ref[...] + bln_ref[...]
    s = ln * _sigmoid(ln)
    return jnp.dot(s.astype(BF16), wpw_ref[...], preferred_element_type=F32)


def _in_proj_prompt_kernel(x_ref, meta_ref, head_ref, g_ref, w_ref, wt_ref, wdw_ref, bdw_ref, gln_ref, bln_ref,
                           wpw_ref, q_ref, gates_ref, conv_ref, kt_ref, vt_ref, glu_tail_ref,
                           h_scr, hs_scr, carry_scr, ext_scr, sh_scr, y_scr, *, tm, n_tiles):
    t = pl.program_id(1)
    d = D_MODEL
    nt = (((1,), (1,)), ((), ()))

    def keys_values():
        hs_scr[0:N_META, :] = carry_scr[...]
        hs_scr[N_META:tm, :] = h_scr[0:tm - N_META, :]
        carry_scr[...] = h_scr[tm - N_META:tm, :]
        hs = hs_scr[...]
        kt_ref[0] = lax.dot_general(wt_ref[0:d, :], hs, nt, preferred_element_type=F32)
        vt_ref[0] = lax.dot_general(wt_ref[d:2 * d, :], hs, nt, preferred_element_type=F32)

    @pl.when(t == 0)
    def _():
        carry_scr[...] = _rms(meta_ref[...], g_ref[...]).astype(BF16)
        ext_scr[0:HALO, :] = head_ref[0]

    @pl.when(jnp.logical_and(t > 0, t < n_tiles))
    def _():
        ext_scr[0:HALO, :] = ext_scr[tm:tm + HALO, :]

    @pl.when(t < n_tiles)
    def _():
        h_scr[...] = _rms(x_ref[0], g_ref[...]).astype(BF16)
        h = h_scr[...]
        proj = lambda c: jnp.dot(h, w_ref[:, c * d:(c + 1) * d], preferred_element_type=F32)
        conv = lambda c: _depthwise_conv_block(c, ext_scr, sh_scr, y_scr, wdw_ref, bdw_ref, tm)
        ext_scr[HALO:HALO + tm, :] = proj(0) * _sigmoid(proj(1))
        glu_tail_ref[0] = ext_scr[tm:tm + HALO, :]
        conv(0)
        conv(1)
        q_ref[0] = (proj(2) * (ATTN_SCALE * LOG2E)).astype(BF16)
        conv(2)
        gates_ref[0, :, :d] = _sigmoid(proj(3)).astype(BF16)
        conv(3)
        gates_ref[0, :, d:] = _sigmoid(proj(4)).astype(BF16)
        conv(4)
        conv(5)
        keys_values()
        conv(6)
        conv(7)
        conv_ref[0] = _ln_swish_pointwise(y_scr[...], gln_ref, bln_ref, wpw_ref).astype(BF16)

    @pl.when(t == n_tiles)
    def _():
        keys_values()


def _in_proj_prompt(x, meta, head, g, w_main, w_kvt, w_dw, b_dw, g_ln, b_ln, w_pw2, tm):
    b, t_len, d = x.shape
    n_tiles = t_len // tm
    row = lambda bi, t: (bi, jnp.minimum(t, n_tiles - 1), 0)
    const = lambda bi, t: (0, 0)
    resident = functools.partial(pl.BlockSpec, index_map=const, pipeline_mode=pl.Buffered(1))
    return pl.pallas_call(
        functools.partial(_in_proj_prompt_kernel, tm=tm, n_tiles=n_tiles),
        grid=(b, n_tiles + 1),
        in_specs=[
            pl.BlockSpec((1, tm, d), row),
            resident((N_META, d)),
            pl.BlockSpec((1, HALO, d), lambda bi, t: (0, 0, 0), pipeline_mode=pl.Buffered(1)),
            resident((1, d)), resident((d, 5 * d)), resident((2 * d, d)),
            resident((CONV_K, d)), resident((1, d)), resident((1, d)), resident((1, d)), resident((d, d)),
        ],
        out_specs=[
            pl.BlockSpec((1, tm, d), row), pl.BlockSpec((1, tm, 2 * d), row), pl.BlockSpec((1, tm, d), row),
            pl.BlockSpec((1, d, tm), lambda bi, t: (bi, 0, t)),
            pl.BlockSpec((1, d, tm), lambda bi, t: (bi, 0, t)),
            pl.BlockSpec((1, HALO, d), lambda bi, t: (bi, 0, 0)),
        ],
        out_shape=[
            jax.ShapeDtypeStruct((b, t_len, d), BF16), jax.ShapeDtypeStruct((b, t_len, 2 * d), BF16),
            jax.ShapeDtypeStruct((b, t_len, d), BF16),
            jax.ShapeDtypeStruct((b, d, N_META + t_len), F32), jax.ShapeDtypeStruct((b, d, N_META + t_len), F32),
            jax.ShapeDtypeStruct((b, HALO, d), F32),
        ],
        scratch_shapes=[pltpu.VMEM((tm, d), BF16), pltpu.VMEM((tm, d), BF16), pltpu.VMEM((N_META, d), BF16),
                        pltpu.VMEM((HALO + tm, d), F32), pltpu.VMEM((2, SUBLANES - 1, HALO + tm, 128), F32),
                        pltpu.VMEM((tm, d), F32)],
        compiler_params=pltpu.CompilerParams(
            dimension_semantics=("arbitrary", "arbitrary"), vmem_limit_bytes=VMEM_LIMIT),
        name="in_proj_prompt",
    )(x, meta, head, g.reshape(1, d), w_main, w_kvt, w_dw, b_dw.reshape(1, d), g_ln.reshape(1, d),
      b_ln.reshape(1, d), w_pw2)


def _conv_kernel(glu_ref, head_ref, wdw_ref, bdw_ref, gln_ref, bln_ref, wpw_ref, out_ref,
                 ext_scr, sh_scr, y_scr, *, tm):
    ext_scr[0:HALO, :] = head_ref[0]
    ext_scr[HALO:HALO + tm, :] = glu_ref[0]
    for c in range(N_CONV_BLOCKS):
        _depthwise_conv_block(c, ext_scr, sh_scr, y_scr, wdw_ref, bdw_ref, tm)
    out_ref[0] = _ln_swish_pointwise(y_scr[...], gln_ref, bln_ref, wpw_ref).astype(BF16)


def _conv_branch(glu, head, w_dw, b_dw, g_ln, b_ln, w_pw2):
    b, tm, d = glu.shape
    const = lambda bi: (0, 0)
    seq = lambda bi: (bi, 0, 0)
    return pl.pallas_call(
        functools.partial(_conv_kernel, tm=tm),
        grid=(b,),
        in_specs=[
            pl.BlockSpec((1, tm, d), seq), pl.BlockSpec((1, HALO, d), seq),
            pl.BlockSpec((CONV_K, d), const),
            pl.BlockSpec((1, d), const), pl.BlockSpec((1, d), const), pl.BlockSpec((1, d), const),
            pl.BlockSpec((d, d), const),
        ],
        out_specs=pl.BlockSpec((1, tm, d), seq),
        out_shape=jax.ShapeDtypeStruct((b, tm, d), BF16),
        scratch_shapes=[pltpu.VMEM((HALO + tm, d), F32), pltpu.VMEM((2, SUBLANES - 1, HALO + tm, 128), F32),
                        pltpu.VMEM((tm, d), F32)],
        compiler_params=pltpu.CompilerParams(dimension_semantics=("arbitrary",), vmem_limit_bytes=VMEM_LIMIT),
        name="conv_branch",
    )(glu, head, w_dw, b_dw.reshape(1, d), g_ln.reshape(1, d), b_ln.reshape(1, d), w_pw2)


def _stack_heads(q):
    lane = lax.broadcasted_iota(jnp.int32, q.shape, 1)
    zero = jnp.zeros_like(q)
    return jnp.concatenate([jnp.where(lane < HEAD_DIM, q, zero), jnp.where(lane >= HEAD_DIM, q, zero)], axis=0)


def _unstack_heads(acc):
    r = acc.shape[0] // 2
    lane = lax.broadcasted_iota(jnp.int32, (r, PAIR), 1)
    return jnp.where(lane < HEAD_DIM, acc[:r], acc[r:])


def _sb_scores(z, mask):
    sp = jnp.maximum(z, 0.0) + jnp.log2(1.0 + jnp.exp2(-jnp.abs(z)))
    ls = z - sp
    if mask is not None:
        sp = jnp.where(mask, sp, 0.0)
        ls = jnp.where(mask, ls, MASKED)
    return ls, sp.astype(BF16)


def _sb_weights(ls, spb, u, r):
    cs = jnp.dot(spb, u, preferred_element_type=F32)
    x = ls - cs
    total = cs[:, 0:1] + spb[:, 0:1].astype(F32)
    mass = jnp.broadcast_to(total, (ls.shape[0], PAIR))
    if r is not None:
        x = x - (r if ls.shape[1] == PAIR else jnp.concatenate([r] * (ls.shape[1] // PAIR), axis=1))
        mass = mass + r
    return jnp.exp2(x).astype(BF16), mass


def _sb_first(z_d, v_d, mask_d, u_d, z_p=None, v_p=None, u_p=None):
    ls_d, spb_d = _sb_scores(z_d, mask_d)
    if z_p is not None:
        ls_p, spb_p = _sb_scores(z_p, None)
    a_d, r = _sb_weights(ls_d, spb_d, u_d, None)
    acc = jnp.dot(a_d, v_d, preferred_element_type=F32)
    if z_p is not None:
        a_p, r = _sb_weights(ls_p, spb_p, u_p, r)
        acc = acc + jnp.dot(a_p, v_p, preferred_element_type=F32)
    return r, acc


def _sb_block(z, v, u, mask, r_ref, acc_ref):
    ls, spb = _sb_scores(z, mask)
    a, r = _sb_weights(ls, spb, u, r_ref[...])
    acc_ref[...] += jnp.dot(a, v, preferred_element_type=F32)
    r_ref[...] = r
    return jnp.min(r)


def _causal_mask(rows_per_head, width):
    row = lax.broadcasted_iota(jnp.int32, (2 * rows_per_head, width), 0)
    col = lax.broadcasted_iota(jnp.int32, (2 * rows_per_head, width), 1)
    return col < jnp.where(row >= rows_per_head, row - rows_per_head, row)


def _meta_mask(rows, width):
    return lax.broadcasted_iota(jnp.int32, (rows, width), 1) < N_META


def _attn_prompt_kernel(q_ref, kt_ref, vt_ref, u256_ref, u128_ref, o_ref,
                        kt_scr, v_scr, kmt_scr, vm_scr, r_scr, acc_scr):
    g = pl.program_id(2)
    tq = KBLK

    @pl.when(g == 0)
    def _():
        for c in range(v_scr.shape[0] // KBLK):
            cols = slice(N_META + c * KBLK, N_META + (c + 1) * KBLK)
            kt_scr[:, c * KBLK:(c + 1) * KBLK] = kt_ref[0, :, cols].astype(BF16)
            v_scr[c * KBLK:(c + 1) * KBLK, :] = vt_ref[0, :, cols].T.astype(BF16)
        kmt_scr[...] = kt_ref[0, :, 0:MBLK].astype(BF16)
        vm_scr[...] = vt_ref[0, :, 0:MBLK].T.astype(BF16)

    u256 = u256_ref[...]
    n_sub = q_ref.shape[1] // tq
    qs = [_stack_heads(q_ref[0, s * tq:(s + 1) * tq, :]) for s in range(n_sub)]

    def scores(s, j):
        return jnp.dot(qs[s], kt_scr[:, pl.ds(pl.multiple_of(j * tq, tq), tq)], preferred_element_type=F32)

    def values(j):
        return v_scr[pl.ds(pl.multiple_of(j * tq, tq), tq), :]

    def first_blocks(first_has_past):
        mask = _causal_mask(tq, tq)
        for s in range(n_sub):
            qi = g * n_sub + s
            if s > 0 or first_has_past:
                r, acc = _sb_first(scores(s, qi), values(qi), mask, u256, scores(s, qi - 1), values(qi - 1), u256)
            else:
                r, acc = _sb_first(scores(s, qi), values(qi), mask, u256)
            r_scr[s] = r
            acc_scr[s] = acc

    @pl.when(g == 0)
    def _():
        first_blocks(False)

    @pl.when(g > 0)
    def _():
        first_blocks(True)

    r_all = r_scr[0]
    for s in range(1, n_sub):
        r_all = jnp.minimum(r_all, r_scr[s])

    @pl.when(jnp.min(r_all) < R_DONE)
    def _():
        for s in range(n_sub):
            qi = g * n_sub + s
            r_ref, acc_ref = r_scr.at[s], acc_scr.at[s]

            def cond(c, qi=qi):
                return jnp.logical_and(c[0] < qi - 1, c[1] < R_DONE)

            def body(c, s=s, qi=qi, r_ref=r_ref, acc_ref=acc_ref):
                j = qi - 2 - c[0]
                return c[0] + 1, _sb_block(scores(s, j), values(j), u256, None, r_ref, acc_ref)

            _, rmin = lax.while_loop(cond, body, (jnp.int32(0), jnp.min(r_ref[...])))

            @pl.when(rmin < R_DONE)
            def _(s=s, r_ref=r_ref, acc_ref=acc_ref):
                z = jnp.dot(qs[s], kmt_scr[...], preferred_element_type=F32)
                _sb_block(z, vm_scr[...], u128_ref[...], _meta_mask(2 * tq, MBLK), r_ref, acc_ref)

    for s in range(n_sub):
        o_ref[0, s * tq:(s + 1) * tq, :] = _unstack_heads(acc_scr[s]).astype(BF16)


def _suffix_sum_matrix(w):
    j = lax.broadcasted_iota(jnp.int32, (w, w), 0)
    s = lax.broadcasted_iota(jnp.int32, (w, w), 1)
    return (j > s).astype(BF16)


def _attn_prompt(q, kt, vt):
    b, t, d = q.shape
    tq = Q_SUB * KBLK
    const2 = lambda bi, hp, g: (0, 0)
    kv_spec = pl.BlockSpec((1, PAIR, N_META + t), lambda bi, hp, g: (bi, hp, 0))
    return pl.pallas_call(
        _attn_prompt_kernel,
        grid=(b, d // PAIR, t // tq),
        in_specs=[
            pl.BlockSpec((1, tq, PAIR), lambda bi, hp, g: (bi, g, hp)),
            kv_spec, kv_spec,
            pl.BlockSpec((KBLK, KBLK), const2),
            pl.BlockSpec((MBLK, MBLK), const2),
        ],
        out_specs=pl.BlockSpec((1, tq, PAIR), lambda bi, hp, g: (bi, g, hp)),
        out_shape=jax.ShapeDtypeStruct((b, t, d), BF16),
        scratch_shapes=[pltpu.VMEM((PAIR, t), BF16), pltpu.VMEM((t, PAIR), BF16),
                        pltpu.VMEM((PAIR, MBLK), BF16), pltpu.VMEM((MBLK, PAIR), BF16),
                        pltpu.VMEM((Q_SUB, 2 * KBLK, PAIR), F32), pltpu.VMEM((Q_SUB, 2 * KBLK, PAIR), F32)],
        compiler_params=pltpu.CompilerParams(
            dimension_semantics=("arbitrary", "arbitrary", "arbitrary"), vmem_limit_bytes=VMEM_LIMIT),
        name="attn_prompt",
    )(q, kt, vt, _suffix_sum_matrix(KBLK), _suffix_sum_matrix(MBLK))


def _attn_sample_kernel(q_ref, kn_ref, vn_ref, ck_ref, cv_ref, km_ref, vm_ref, u256_ref, u128_ref, o_ref,
                        r_scr, acc_scr, *, s_len, n_past):
    qs = _stack_heads(q_ref[0])
    u256 = u256_ref[...]
    u128 = u128_ref[...]
    nt = (((1,), (1,)), ((), ()))

    def cache_block(j):
        start = pl.multiple_of(j * KBLK, KBLK)
        z = jnp.dot(qs, ck_ref[0, :, pl.ds(start, KBLK)].astype(BF16), preferred_element_type=F32)
        return z, cv_ref[0, :, pl.ds(start, KBLK)].T.astype(BF16)

    z_new = lax.dot_general(qs, kn_ref[0], nt, preferred_element_type=F32)
    z_past, v_past = cache_block(n_past - 1)
    r, acc = _sb_first(z_new, vn_ref[0], _causal_mask(s_len, MBLK), u128, z_past, v_past, u256)
    r_scr[...] = r
    acc_scr[...] = acc

    def cond(c):
        return jnp.logical_and(c[0] < n_past - 1, c[1] < R_DONE)

    def body(c):
        z, v = cache_block(n_past - 2 - c[0])
        return c[0] + 1, _sb_block(z, v, u256, None, r_scr, acc_scr)

    _, rmin = lax.while_loop(cond, body, (jnp.int32(0), jnp.min(r)))

    @pl.when(rmin < R_DONE)
    def _():
        z = lax.dot_general(qs, km_ref[...], nt, preferred_element_type=F32)
        _sb_block(z, vm_ref[...], u128, _meta_mask(2 * s_len, MBLK), r_scr, acc_scr)

    o_ref[0] = _unstack_heads(acc_scr[...]).astype(BF16)


def _attn_sample(q, k_new, v_new, cache_k, cache_v, k_meta, v_meta):
    b, s_len, d = q.shape
    p = cache_k.shape[2]
    blk = lambda bi, hp: (bi, 0, hp)
    const2 = lambda bi, hp: (0, 0)
    cache_spec = pl.BlockSpec((1, PAIR, p), lambda bi, hp: (bi, hp, 0))
    return pl.pallas_call(
        functools.partial(_attn_sample_kernel, s_len=s_len, n_past=p // KBLK),
        grid=(b, d // PAIR),
        in_specs=[
            pl.BlockSpec((1, s_len, PAIR), blk),
            pl.BlockSpec((1, MBLK, PAIR), blk), pl.BlockSpec((1, MBLK, PAIR), blk),
            cache_spec, cache_spec,
            pl.BlockSpec((MBLK, PAIR), lambda bi, hp: (0, hp)),
            pl.BlockSpec((MBLK, PAIR), lambda bi, hp: (0, hp)),
            pl.BlockSpec((KBLK, KBLK), const2),
            pl.BlockSpec((MBLK, MBLK), const2),
        ],
        out_specs=pl.BlockSpec((1, s_len, PAIR), blk),
        out_shape=jax.ShapeDtypeStruct((b, s_len, d), BF16),
        scratch_shapes=[pltpu.VMEM((2 * s_len, PAIR), F32), pltpu.VMEM((2 * s_len, PAIR), F32)],
        compiler_params=pltpu.CompilerParams(
            dimension_semantics=("arbitrary", "arbitrary"), vmem_limit_bytes=VMEM_LIMIT),
        name="attn_sample",
    )(q, k_new, v_new, cache_k, cache_v, k_meta, v_meta, _suffix_sum_matrix(KBLK), _suffix_sum_matrix(MBLK))


def _merge_mlp_kernel(x_ref, conv_ref, attn_ref, gates_ref, wout_ref, gmlp_ref, wup_ref, wdown_ref, gfin_ref,
                      y_ref):
    g = gates_ref[...]
    mixed = (g[:, :D_MODEL].astype(F32) * conv_ref[...].astype(F32)
             + g[:, D_MODEL:].astype(F32) * attn_ref[...].astype(F32))
    x1 = x_ref[...] + jnp.dot(mixed.astype(BF16), wout_ref[...], preferred_element_type=F32)
    h = _rms(x1, gmlp_ref[...]).astype(BF16)
    acc = x1
    for c in range(D_FF // D_MODEL):
        cs = slice(c * D_MODEL, (c + 1) * D_MODEL)
        u = jnp.maximum(jnp.dot(h, wup_ref[:, cs], preferred_element_type=F32), 0.0)
        acc = acc + jnp.dot((u * u).astype(BF16), wdown_ref[cs, :], preferred_element_type=F32)
    y_ref[...] = _rms(acc, gfin_ref[...])


def _merge_mlp(x, conv, attn, gates, w_out, g_mlp, w_up, w_down, g_final, tm):
    m, d = x.shape
    row = lambda i: (i, 0)
    const = lambda i: (0, 0)
    resident = functools.partial(pl.BlockSpec, index_map=const, pipeline_mode=pl.Buffered(1))
    return pl.pallas_call(
        _merge_mlp_kernel,
        grid=(m // tm,),
        in_specs=[
            pl.BlockSpec((tm, d), row), pl.BlockSpec((tm, d), row), pl.BlockSpec((tm, d), row),
            pl.BlockSpec((tm, 2 * d), row),
            resident((d, d)), resident((1, d)), resident((d, D_FF)), resident((D_FF, d)), resident((1, d)),
        ],
        out_specs=pl.BlockSpec((tm, d), row),
        out_shape=jax.ShapeDtypeStruct((m, d), F32),
        compiler_params=pltpu.CompilerParams(
            dimension_semantics=("arbitrary",), vmem_limit_bytes=VMEM_LIMIT),
        name="merge_mlp",
    )(x, conv, attn, gates, w_out, g_mlp.reshape(1, d), w_up, w_down, g_final.reshape(1, d))


def kernel(x_prompt, x_sample, cache_k, cache_v, cache_conv, meta, g_mix, w_in, w_dw, b_dw, g_ln_conv,
           b_ln_conv, w_pw2, w_out, g_mlp, w_up, w_down, g_final):
    b, t, d = x_prompt.shape
    sb, s_len, _ = x_sample.shape
    depth, _, past, _, _ = cache_k.shape
    assert depth == 1 and d == D_MODEL and meta.shape == (N_META, d)
    assert t % 512 == 0 and past % KBLK == 0 and s_len % 16 == 0 and CONV_K - 1 <= s_len <= MBLK
    n_s = sb * s_len

    w_in_b = w_in[0].astype(BF16)
    w_pw2_b = w_pw2[0].astype(BF16)
    w_out_b = w_out[0].astype(BF16)
    w_up_b = w_up[0].astype(BF16)
    w_down_b = w_down[0].astype(BF16)

    x_sm = jnp.concatenate([x_sample.reshape(n_s, d), meta.astype(F32)], axis=0)
    glu_s, q_s, k_s, kb_s, v_s, vb_s, gates_s = _in_proj(x_sm, g_mix[0], w_in_b, tm=n_s + N_META)

    pad_meta = ((0, MBLK - N_META), (0, 0))
    kb_meta = jnp.pad(kb_s[n_s:], pad_meta)
    vb_meta = jnp.pad(vb_s[n_s:], pad_meta)

    conv_w = (w_dw[0], b_dw[0], g_ln_conv[0], b_ln_conv[0], w_pw2_b)
    w_main_b = jnp.concatenate([w_in_b[:, :3 * d], w_in_b[:, 5 * d:]], axis=1)
    w_kvt_b = w_in_b[:, 3 * d:5 * d].T
    head_p = jnp.pad(glu_s[n_s:], ((HALO - N_META, 0), (0, 0)))[None]
    q_p, gates_p, conv_p, kt_p, vt_p, glu_tail_p = _in_proj_prompt(
        x_prompt, meta.astype(F32), head_p, g_mix[0], w_main_b, w_kvt_b, *conv_w, tm=512)

    glu_s3 = glu_s[:n_s].reshape(sb, s_len, d)
    head_s = jnp.pad(cache_conv[0], ((0, 0), (HALO - (CONV_K - 1), 0), (0, 0)))
    conv_s = _conv_branch(glu_s3, head_s, *conv_w)

    attn_p = _attn_prompt(q_p, kt_p, vt_p)
    pad_new = ((0, 0), (0, MBLK - s_len), (0, 0))
    feature_major = lambda c: c.transpose(0, 2, 3, 1).reshape(sb, d, past)
    attn_s = _attn_sample(
        q_s[:n_s].reshape(sb, s_len, d),
        jnp.pad(kb_s[:n_s].reshape(sb, s_len, d), pad_new), jnp.pad(vb_s[:n_s].reshape(sb, s_len, d), pad_new),
        feature_major(cache_k[0]), feature_major(cache_v[0]), kb_meta, vb_meta)

    mlp_w = (w_out_b, g_mlp[0], w_up_b, w_down_b, g_final)
    y_p = _merge_mlp(x_prompt.reshape(b * t, d), conv_p.reshape(b * t, d), attn_p.reshape(b * t, d),
                     gates_p.reshape(b * t, 2 * d), *mlp_w, tm=512)
    y_s = _merge_mlp(x_sample.reshape(n_s, d), conv_s.reshape(n_s, d), attn_s.reshape(n_s, d), gates_s[:n_s],
                     *mlp_w, tm=n_s)

    def key_major(xt):
        return xt.reshape(1, b, N_HEADS, HEAD_DIM, N_META + t).transpose(0, 1, 4, 2, 3)

    return (
        y_p.reshape(b, t, d),
        y_s.reshape(sb, s_len, d),
        key_major(kt_p),
        key_major(vt_p),
        glu_tail_p[:, HALO - (CONV_K - 1):][None],
        k_s[:n_s].reshape(1, sb, s_len, N_HEADS, HEAD_DIM),
        v_s[:n_s].reshape(1, sb, s_len, N_HEADS, HEAD_DIM),
        glu_s3[:, s_len - (CONV_K - 1):][None],
    )
```

```python
import functools

import jax
import jax.numpy as jnp
from jax import lax
from jax.experimental import pallas as pl
from jax.experimental.pallas import tpu as pltpu

F32 = jnp.float32
BF16 = jnp.bfloat16

D_MODEL = 1024
N_META = 16
CONV_K = 31
N_HEADS = 16
HEAD_DIM = 64
D_FF = 4 * D_MODEL
EPS = 1e-6
ATTN_SCALE = HEAD_DIM ** -0.5
LOG2E = 1.4426950408889634
PAIR = 2 * HEAD_DIM
SUBLANES = 8
HALO = 32
KBLK = 256
MBLK = 128
Q_SUB = 4
RECENT_BLOCKS = 2
MASKED = -1e30
R_DONE = 160.0
VMEM_LIMIT = 56 * 1024 * 1024


def _sigmoid(x):
    return 1.0 / (1.0 + jnp.exp(-x))


def _rms(x, g):
    return x * lax.rsqrt(jnp.mean(x * x, axis=-1, keepdims=True) + EPS) * g


def _in_proj_kernel(x_ref, g_ref, w_ref, wb_ref, glu_ref, q_ref, k_ref, kb_ref, v_ref, vb_ref,
                    gates_ref, h_scr):
    j = pl.program_id(1)

    @pl.when(j == 0)
    def _():
        h_scr[...] = _rms(x_ref[...], g_ref[...]).astype(BF16)

    h = h_scr[...]
    p = jnp.dot(h, w_ref[...], preferred_element_type=F32)

    @pl.when(j == 0)
    def _():
        b = jnp.dot(h, wb_ref[...], preferred_element_type=F32)
        glu_ref[...] = p * _sigmoid(b)

    @pl.when(j == 1)
    def _():
        q_ref[...] = (p * (ATTN_SCALE * LOG2E)).astype(BF16)

    @pl.when(j == 2)
    def _():
        k_ref[...] = p
        kb_ref[...] = p.astype(BF16)

    @pl.when(j == 3)
    def _():
        v_ref[...] = p
        vb_ref[...] = p.astype(BF16)

    @pl.when(j == 4)
    def _():
        gates_ref[:, :D_MODEL] = _sigmoid(p).astype(BF16)

    @pl.when(j == 5)
    def _():
        gates_ref[:, D_MODEL:] = _sigmoid(p).astype(BF16)


def _in_proj(x, g, w_in, tm):
    m = x.shape[0]
    d = D_MODEL
    row = lambda i, j: (i, 0)
    return pl.pallas_call(
        _in_proj_kernel,
        grid=(m // tm, 6),
        in_specs=[
            pl.BlockSpec((tm, d), row),
            pl.BlockSpec((1, d), lambda i, j: (0, 0)),
            pl.BlockSpec((d, d), lambda i, j: (0, jnp.where(j == 0, 0, j + 1))),
            pl.BlockSpec((d, d), lambda i, j: (0, 1)),
        ],
        out_specs=[
            pl.BlockSpec((tm, d), row), pl.BlockSpec((tm, d), row),
            pl.BlockSpec((tm, d), row), pl.BlockSpec((tm, d), row),
            pl.BlockSpec((tm, d), row), pl.BlockSpec((tm, d), row),
            pl.BlockSpec((tm, 2 * d), row),
        ],
        out_shape=[
            jax.ShapeDtypeStruct((m, d), F32), jax.ShapeDtypeStruct((m, d), BF16),
            jax.ShapeDtypeStruct((m, d), F32), jax.ShapeDtypeStruct((m, d), BF16),
            jax.ShapeDtypeStruct((m, d), F32), jax.ShapeDtypeStruct((m, d), BF16),
            jax.ShapeDtypeStruct((m, 2 * d), BF16),
        ],
        scratch_shapes=[pltpu.VMEM((tm, d), BF16)],
        compiler_params=pltpu.CompilerParams(
            dimension_semantics=("arbitrary", "arbitrary"), vmem_limit_bytes=VMEM_LIMIT),
        name="in_proj",
    )(x, g.reshape(1, d), w_in, w_in)


def _transpose_kernel(w_ref, o_ref):
    o_ref[...] = w_ref[...].T.astype(BF16)


def _transposed_columns(w, first, count, tn=512):
    k = w.shape[0]
    return pl.pallas_call(
        _transpose_kernel,
        grid=(count // tn,),
        in_specs=[pl.BlockSpec((k, tn), lambda j: (0, first // tn + j))],
        out_specs=pl.BlockSpec((tn, k), lambda j: (j, 0)),
        out_shape=jax.ShapeDtypeStruct((count, k), BF16),
        compiler_params=pltpu.CompilerParams(dimension_semantics=("arbitrary",), vmem_limit_bytes=VMEM_LIMIT),
        name="transpose_weights",
    )(w)


def _depthwise_conv_block(c, ext_scr, sh_scr, y_scr, wdw_ref, bdw_ref, tm):
    rows = HALO + tm - SUBLANES
    rc = min(tm, 64)
    first = HALO - (CONV_K - 1)
    cs = slice(c * 128, (c + 1) * 128)
    sh = sh_scr.at[c % 2]
    for m in range(1, SUBLANES):
        sh[m - 1, 0:rows, :] = ext_scr[m:m + rows, cs]
    for r in range(tm // rc):
        acc = jnp.broadcast_to(bdw_ref[:, cs], (rc, 128))
        for t in range(CONV_K):
            m = (t + first) % SUBLANES
            lo = r * rc + (t + first) - m
            src = ext_scr[lo:lo + rc, cs] if m == 0 else sh[m - 1, lo:lo + rc, :]
            acc = acc + wdw_ref[t:t + 1, cs] * src
        y_scr[r * rc:(r + 1) * rc, cs] = acc


N_CONV_BLOCKS = D_MODEL // 128


def _ln_swish_pointwise(y, gln_ref, bln_ref, wpw_ref):
    yc = y - jnp.mean(y, axis=-1, keepdims=True)
    var = jnp.mean(yc * yc, axis=-1, keepdims=True)
    ln = yc * lax.rsqrt(var + EPS) * gln_ref[...] + bln_ref[...]
    s = ln * _sigmoid(ln)
    return jnp.dot(s.astype(BF16), wpw_ref[...], preferred_element_type=F32)


def _in_proj_prompt_kernel(x_ref, meta_ref, head_ref, g_ref, w_ref, wt_ref, wdw_ref, bdw_ref, gln_ref, bln_ref,
                           wpw_ref, q_ref, gates_ref, conv_ref, kt_ref, vt_ref, glu_tail_ref,
                           h_scr, hs_scr, carry_scr, ext_scr, sh_scr, y_scr, *, tm, n_tiles):
    t = pl.program_id(1)
    d = D_MODEL
    nt = (((1,), (1,)), ((), ()))

    def keys_values():
        hs_scr[0:N_META, :] = carry_scr[...]
        hs_scr[N_META:tm, :] = h_scr[0:tm - N_META, :]
        carry_scr[...] = h_scr[tm - N_META:tm, :]
        hs = hs_scr[...]
        kt_ref[0] = lax.dot_general(wt_ref[0:d, :], hs, nt, preferred_element_type=F32)
        vt_ref[0] = lax.dot_general(wt_ref[d:2 * d, :], hs, nt, preferred_element_type=F32)

    @pl.when(t == 0)
    def _():
        carry_scr[...] = _rms(meta_ref[...], g_ref[...]).astype(BF16)
        ext_scr[0:HALO, :] = head_ref[0]

    @pl.when(jnp.logical_and(t > 0, t < n_tiles))
    def _():
        ext_scr[0:HALO, :] = ext_scr[tm:tm + HALO, :]

    @pl.when(t < n_tiles)
    def _():
        h_scr[...] = _rms(x_ref[0], g_ref[...]).astype(BF16)
        h = h_scr[...]
        proj = lambda c: jnp.dot(h, w_ref[:, c * d:(c + 1) * d], preferred_element_type=F32)
        conv = lambda c: _depthwise_conv_block(c, ext_scr, sh_scr, y_scr, wdw_ref, bdw_ref, tm)
        ext_scr[HALO:HALO + tm, :] = proj(0) * _sigmoid(proj(1))
        glu_tail_ref[0] = ext_scr[tm:tm + HALO, :]
        conv(0)
        conv(1)
        q_ref[0] = (proj(2) * (ATTN_SCALE * LOG2E)).astype(BF16)
        conv(2)
        gates_ref[0, :, :d] = _sigmoid(proj(3)).astype(BF16)
        conv(3)
        gates_ref[0, :, d:] = _sigmoid(proj(4)).astype(BF16)
        conv(4)
        conv(5)
        keys_values()
        conv(6)
        conv(7)
        conv_ref[0] = _ln_swish_pointwise(y_scr[...], gln_ref, bln_ref, wpw_ref).astype(BF16)

    @pl.when(t == n_tiles)
    def _():
        keys_values()


def _in_proj_prompt(x, meta, head, g, w_main, w_kvt, w_dw, b_dw, g_ln, b_ln, w_pw2, tm):
    b, t_len, d = x.shape
    n_tiles = t_len // tm
    row = lambda bi, t: (bi, jnp.minimum(t, n_tiles - 1), 0)
    const = lambda bi, t: (0, 0)
    resident = functools.partial(pl.BlockSpec, index_map=const, pipeline_mode=pl.Buffered(1))
    return pl.pallas_call(
        functools.partial(_in_proj_prompt_kernel, tm=tm, n_tiles=n_tiles),
        grid=(b, n_tiles + 1),
        in_specs=[
            pl.BlockSpec((1, tm, d), row),
            resident((N_META, d)),
            pl.BlockSpec((1, HALO, d), lambda bi, t: (0, 0, 0), pipeline_mode=pl.Buffered(1)),
            resident((1, d)), resident((d, 5 * d)), resident((2 * d, d)),
            resident((CONV_K, d)), resident((1, d)), resident((1, d)), resident((1, d)), resident((d, d)),
        ],
        out_specs=[
            pl.BlockSpec((1, tm, d), row), pl.BlockSpec((1, tm, 2 * d), row), pl.BlockSpec((1, tm, d), row),
            pl.BlockSpec((1, d, tm), lambda bi, t: (bi, 0, t)),
            pl.BlockSpec((1, d, tm), lambda bi, t: (bi, 0, t)),
            pl.BlockSpec((1, HALO, d), lambda bi, t: (bi, 0, 0)),
        ],
        out_shape=[
            jax.ShapeDtypeStruct((b, t_len, d), BF16), jax.ShapeDtypeStruct((b, t_len, 2 * d), BF16),
            jax.ShapeDtypeStruct((b, t_len, d), BF16),
            jax.ShapeDtypeStruct((b, d, N_META + t_len), F32), jax.ShapeDtypeStruct((b, d, N_META + t_len), F32),
            jax.ShapeDtypeStruct((b, HALO, d), F32),
        ],
        scratch_shapes=[pltpu.VMEM((tm, d), BF16), pltpu.VMEM((tm, d), BF16), pltpu.VMEM((N_META, d), BF16),
                        pltpu.VMEM((HALO + tm, d), F32), pltpu.VMEM((2, SUBLANES - 1, HALO + tm, 128), F32),
                        pltpu.VMEM((tm, d), F32)],
        compiler_params=pltpu.CompilerParams(
            dimension_semantics=("arbitrary", "arbitrary"), vmem_limit_bytes=VMEM_LIMIT),
        name="in_proj_prompt",
    )(x, meta, head, g.reshape(1, d), w_main, w_kvt, w_dw, b_dw.reshape(1, d), g_ln.reshape(1, d),
      b_ln.reshape(1, d), w_pw2)


def _conv_kernel(glu_ref, head_ref, wdw_ref, bdw_ref, gln_ref, bln_ref, wpw_ref, out_ref,
                 ext_scr, sh_scr, y_scr, *, tm):
    ext_scr[0:HALO, :] = head_ref[0]
    ext_scr[HALO:HALO + tm, :] = glu_ref[0]
    for c in range(N_CONV_BLOCKS):
        _depthwise_conv_block(c, ext_scr, sh_scr, y_scr, wdw_ref, bdw_ref, tm)
    out_ref[0] = _ln_swish_pointwise(y_scr[...], gln_ref, bln_ref, wpw_ref).astype(BF16)


def _conv_branch(glu, head, w_dw, b_dw, g_ln, b_ln, w_pw2):
    b, tm, d = glu.shape
    const = lambda bi: (0, 0)
    seq = lambda bi: (bi, 0, 0)
    return pl.pallas_call(
        functools.partial(_conv_kernel, tm=tm),
        grid=(b,),
        in_specs=[
            pl.BlockSpec((1, tm, d), seq), pl.BlockSpec((1, HALO, d), seq),
            pl.BlockSpec((CONV_K, d), const),
            pl.BlockSpec((1, d), const), pl.BlockSpec((1, d), const), pl.BlockSpec((1, d), const),
            pl.BlockSpec((d, d), const),
        ],
        out_specs=pl.BlockSpec((1, tm, d), seq),
        out_shape=jax.ShapeDtypeStruct((b, tm, d), BF16),
        scratch_shapes=[pltpu.VMEM((HALO + tm, d), F32), pltpu.VMEM((2, SUBLANES - 1, HALO + tm, 128), F32),
                        pltpu.VMEM((tm, d), F32)],
        compiler_params=pltpu.CompilerParams(dimension_semantics=("arbitrary",), vmem_limit_bytes=VMEM_LIMIT),
        name="conv_branch",
    )(glu, head, w_dw, b_dw.reshape(1, d), g_ln.reshape(1, d), b_ln.reshape(1, d), w_pw2)


def _stack_heads(q):
    lane = lax.broadcasted_iota(jnp.int32, q.shape, 1)
    zero = jnp.zeros_like(q)
    return jnp.concatenate([jnp.where(lane < HEAD_DIM, q, zero), jnp.where(lane >= HEAD_DIM, q, zero)], axis=0)


def _unstack_heads(acc):
    r = acc.shape[0] // 2
    lane = lax.broadcasted_iota(jnp.int32, (r, PAIR), 1)
    return jnp.where(lane < HEAD_DIM, acc[:r], acc[r:])


def _sb_scores(z, mask):
    sp = jnp.maximum(z, 0.0) + jnp.log2(1.0 + jnp.exp2(-jnp.abs(z)))
    ls = z - sp
    if mask is not None:
        sp = jnp.where(mask, sp, 0.0)
        ls = jnp.where(mask, ls, MASKED)
    return ls, sp.astype(BF16)


def _sb_weights(ls, spb, u, r):
    cs = jnp.dot(spb, u, preferred_element_type=F32)
    x = ls - cs
    total = cs[:, 0:1] + spb[:, 0:1].astype(F32)
    mass = jnp.broadcast_to(total, (ls.shape[0], PAIR))
    if r is not None:
        x = x - (r if ls.shape[1] == PAIR else jnp.concatenate([r] * (ls.shape[1] // PAIR), axis=1))
        mass = mass + r
    return jnp.exp2(x).astype(BF16), mass


def _sb_first(z_d, v_d, mask_d, u_d, z_p=None, v_p=None, u_p=None):
    ls_d, spb_d = _sb_scores(z_d, mask_d)
    if z_p is not None:
        ls_p, spb_p = _sb_scores(z_p, None)
    a_d, r = _sb_weights(ls_d, spb_d, u_d, None)
    acc = jnp.dot(a_d, v_d, preferred_element_type=F32)
    if z_p is not None:
        a_p, r = _sb_weights(ls_p, spb_p, u_p, r)
        acc = acc + jnp.dot(a_p, v_p, preferred_element_type=F32)
    return r, acc


def _sb_block(z, v, u, mask, r_ref, acc_ref):
    ls, spb = _sb_scores(z, mask)
    a, r = _sb_weights(ls, spb, u, r_ref[...])
    acc_ref[...] += jnp.dot(a, v, preferred_element_type=F32)
    r_ref[...] = r
    return jnp.min(r)


def _causal_mask(rows_per_head, width):
    row = lax.broadcasted_iota(jnp.int32, (2 * rows_per_head, width), 0)
    col = lax.broadcasted_iota(jnp.int32, (2 * rows_per_head, width), 1)
    return col < jnp.where(row >= rows_per_head, row - rows_per_head, row)


def _meta_mask(rows, width):
    return lax.broadcasted_iota(jnp.int32, (rows, width), 1) < N_META


def _attn_prompt_kernel(q_ref, kt_ref, vt_ref, u256_ref, u128_ref, o_ref,
                        kt_scr, v_scr, kmt_scr, vm_scr, r_scr, acc_scr):
    g = pl.program_id(2)
    tq = KBLK

    @pl.when(g == 0)
    def _():
        for c in range(v_scr.shape[0] // KBLK):
            cols = slice(N_META + c * KBLK, N_META + (c + 1) * KBLK)
            kt_scr[:, c * KBLK:(c + 1) * KBLK] = kt_ref[0, :, cols].astype(BF16)
            v_scr[c * KBLK:(c + 1) * KBLK, :] = vt_ref[0, :, cols].T.astype(BF16)
        kmt_scr[...] = kt_ref[0, :, 0:MBLK].astype(BF16)
        vm_scr[...] = vt_ref[0, :, 0:MBLK].T.astype(BF16)

    u256 = u256_ref[...]
    n_sub = q_ref.shape[1] // tq
    qs = [_stack_heads(q_ref[0, s * tq:(s + 1) * tq, :]) for s in range(n_sub)]

    def scores(s, j):
        return jnp.dot(qs[s], kt_scr[:, pl.ds(pl.multiple_of(j * tq, tq), tq)], preferred_element_type=F32)

    def values(j):
        return v_scr[pl.ds(pl.multiple_of(j * tq, tq), tq), :]

    def first_blocks(first_has_past):
        mask = _causal_mask(tq, tq)
        for s in range(n_sub):
            qi = g * n_sub + s
            if s > 0 or first_has_past:
                r, acc = _sb_first(scores(s, qi), values(qi), mask, u256, scores(s, qi - 1), values(qi - 1), u256)
            else:
                r, acc = _sb_first(scores(s, qi), values(qi), mask, u256)
            r_scr[s] = r
            acc_scr[s] = acc

    @pl.when(g == 0)
    def _():
        first_blocks(False)

    @pl.when(g > 0)
    def _():
        first_blocks(True)

    r_all = r_scr[0]
    for s in range(1, n_sub):
        r_all = jnp.minimum(r_all, r_scr[s])

    @pl.when(jnp.min(r_all) < R_DONE)
    def _():
        for s in range(n_sub):
            qi = g * n_sub + s
            r_ref, acc_ref = r_scr.at[s], acc_scr.at[s]

            def cond(c, qi=qi):
                return jnp.logical_and(c[0] < qi - 1, c[1] < R_DONE)

            def body(c, s=s, qi=qi, r_ref=r_ref, acc_ref=acc_ref):
                j = qi - 2 - c[0]
                return c[0] + 1, _sb_block(scores(s, j), values(j), u256, None, r_ref, acc_ref)

            _, rmin = lax.while_loop(cond, body, (jnp.int32(0), jnp.min(r_ref[...])))

            @pl.when(rmin < R_DONE)
            def _(s=s, r_ref=r_ref, acc_ref=acc_ref):
                z = jnp.dot(qs[s], kmt_scr[...], preferred_element_type=F32)
                _sb_block(z, vm_scr[...], u128_ref[...], _meta_mask(2 * tq, MBLK), r_ref, acc_ref)

    for s in range(n_sub):
        o_ref[0, s * tq:(s + 1) * tq, :] = _unstack_heads(acc_scr[s]).astype(BF16)


def _suffix_sum_matrix(w):
    j = lax.broadcasted_iota(jnp.int32, (w, w), 0)
    s = lax.broadcasted_iota(jnp.int32, (w, w), 1)
    return (j > s).astype(BF16)


def _attn_prompt(q, kt, vt):
    b, t, d = q.shape
    tq = Q_SUB * KBLK
    const2 = lambda bi, hp, g: (0, 0)
    kv_spec = pl.BlockSpec((1, PAIR, N_META + t), lambda bi, hp, g: (bi, hp, 0))
    return pl.pallas_call(
        _attn_prompt_kernel,
        grid=(b, d // PAIR, t // tq),
        in_specs=[
            pl.BlockSpec((1, tq, PAIR), lambda bi, hp, g: (bi, g, hp)),
            kv_spec, kv_spec,
            pl.BlockSpec((KBLK, KBLK), const2),
            pl.BlockSpec((MBLK, MBLK), const2),
        ],
        out_specs=pl.BlockSpec((1, tq, PAIR), lambda bi, hp, g: (bi, g, hp)),
        out_shape=jax.ShapeDtypeStruct((b, t, d), BF16),
        scratch_shapes=[pltpu.VMEM((PAIR, t), BF16), pltpu.VMEM((t, PAIR), BF16),
                        pltpu.VMEM((PAIR, MBLK), BF16), pltpu.VMEM((MBLK, PAIR), BF16),
                        pltpu.VMEM((Q_SUB, 2 * KBLK, PAIR), F32), pltpu.VMEM((Q_SUB, 2 * KBLK, PAIR), F32)],
        compiler_params=pltpu.CompilerParams(
            dimension_semantics=("arbitrary", "arbitrary", "arbitrary"), vmem_limit_bytes=VMEM_LIMIT),
        name="attn_prompt",
    )(q, kt, vt, _suffix_sum_matrix(KBLK), _suffix_sum_matrix(MBLK))


def _attn_sample_kernel(q_ref, kn_ref, vn_ref, ck_ref, cv_ref, u256_ref, u128_ref, o_ref, r_ref, acc_ref,
                        *, s_len, n_pairs):
    u256 = u256_ref[...]
    u128 = u128_ref[...]
    nt = (((1,), (1,)), ((), ()))
    lanes = [slice(p * PAIR, (p + 1) * PAIR) for p in range(n_pairs)]
    qs = [_stack_heads(q_ref[0, :, ln]) for ln in lanes]

    def cache_block(p, j):
        cols = slice(j * KBLK, (j + 1) * KBLK)
        z = jnp.dot(qs[p], ck_ref[0, lanes[p], cols].astype(BF16), preferred_element_type=F32)
        return z, cv_ref[0, lanes[p], cols].T.astype(BF16)

    mask = _causal_mask(s_len, MBLK)
    r_all = None
    for p in range(n_pairs):
        z_new = lax.dot_general(qs[p], kn_ref[0, :, lanes[p]], nt, preferred_element_type=F32)
        z_past, v_past = cache_block(p, RECENT_BLOCKS - 1)
        r, acc = _sb_first(z_new, vn_ref[0, :, lanes[p]], mask, u128, z_past, v_past, u256)
        r_ref[0, p] = r
        acc_ref[0, p] = acc
        r_all = r if r_all is None else jnp.minimum(r_all, r)

    @pl.when(jnp.min(r_all) < R_DONE)
    def _():
        for p in range(n_pairs):
            for j in range(RECENT_BLOCKS - 2, -1, -1):
                z, v = cache_block(p, j)
                _sb_block(z, v, u256, None, r_ref.at[0, p], acc_ref.at[0, p])

    for p in range(n_pairs):
        o_ref[0, :, lanes[p]] = _unstack_heads(acc_ref[0, p]).astype(BF16)


def _attn_sample_rest_kernel(q_ref, ck_ref, cv_ref, km_ref, vm_ref, r_in_ref, acc_in_ref, u256_ref, u128_ref,
                             o_ref, r_scr, acc_scr, *, s_len, n_rest):
    qs = _stack_heads(q_ref[0])
    u256 = u256_ref[...]
    nt = (((1,), (1,)), ((), ()))
    r_scr[...] = r_in_ref[0, 0]
    acc_scr[...] = acc_in_ref[0, 0]

    def cond(c):
        return jnp.logical_and(c[0] < n_rest, c[1] < R_DONE)

    def body(c):
        start = pl.multiple_of((n_rest - 1 - c[0]) * KBLK, KBLK)
        z = jnp.dot(qs, ck_ref[0, :, pl.ds(start, KBLK)].astype(BF16), preferred_element_type=F32)
        v = cv_ref[0, :, pl.ds(start, KBLK)].T.astype(BF16)
        return c[0] + 1, _sb_block(z, v, u256, None, r_scr, acc_scr)

    _, rmin = lax.while_loop(cond, body, (jnp.int32(0), jnp.min(r_scr[...])))

    @pl.when(rmin < R_DONE)
    def _():
        z = lax.dot_general(qs, km_ref[...], nt, preferred_element_type=F32)
        _sb_block(z, vm_ref[...], u128_ref[...], _meta_mask(2 * s_len, MBLK), r_scr, acc_scr)

    o_ref[0] = _unstack_heads(acc_scr[...]).astype(BF16)


def _attn_sample(q, k_new, v_new, cache_k, cache_v, k_meta, v_meta):
    b, s_len, d = q.shape
    p = cache_k.shape[2]
    n_pairs = 2
    n_groups = d // (n_pairs * PAIR)
    recent = RECENT_BLOCKS * KBLK
    grp = lambda bi, h: (bi, 0, h)
    const2 = lambda bi, h: (0, 0)
    state = jax.ShapeDtypeStruct((b, d // PAIR, 2 * s_len, PAIR), F32)
    state_spec = pl.BlockSpec((1, n_pairs, 2 * s_len, PAIR), lambda bi, h: (bi, h, 0, 0))
    recent_spec = pl.BlockSpec((1, n_pairs * PAIR, recent), lambda bi, h: (bi, h, p // recent - 1))
    u256, u128 = _suffix_sum_matrix(KBLK), _suffix_sum_matrix(MBLK)
    o_recent, r_state, acc_state = pl.pallas_call(
        functools.partial(_attn_sample_kernel, s_len=s_len, n_pairs=n_pairs),
        grid=(b, n_groups),
        in_specs=[
            pl.BlockSpec((1, s_len, n_pairs * PAIR), grp),
            pl.BlockSpec((1, MBLK, n_pairs * PAIR), grp), pl.BlockSpec((1, MBLK, n_pairs * PAIR), grp),
            recent_spec, recent_spec,
            pl.BlockSpec((KBLK, KBLK), const2), pl.BlockSpec((MBLK, MBLK), const2),
        ],
        out_specs=[pl.BlockSpec((1, s_len, n_pairs * PAIR), grp), state_spec, state_spec],
        out_shape=[jax.ShapeDtypeStruct((b, s_len, d), BF16), state, state],
        compiler_params=pltpu.CompilerParams(
            dimension_semantics=("arbitrary", "arbitrary"), vmem_limit_bytes=VMEM_LIMIT),
        name="attn_sample",
    )(q, k_new, v_new, cache_k, cache_v, u256, u128)

    def rest():
        pair = lambda bi, hp: (bi, 0, hp)
        one_state = pl.BlockSpec((1, 1, 2 * s_len, PAIR), lambda bi, hp: (bi, hp, 0, 0))
        cache_spec = pl.BlockSpec((1, PAIR, p), lambda bi, hp: (bi, hp, 0))
        return pl.pallas_call(
            functools.partial(_attn_sample_rest_kernel, s_len=s_len, n_rest=p // KBLK - RECENT_BLOCKS),
            grid=(b, d // PAIR),
            in_specs=[
                pl.BlockSpec((1, s_len, PAIR), pair),
                cache_spec, cache_spec,
                pl.BlockSpec((MBLK, PAIR), lambda bi, hp: (0, hp)),
                pl.BlockSpec((MBLK, PAIR), lambda bi, hp: (0, hp)),
                one_state, one_state,
                pl.BlockSpec((KBLK, KBLK), const2), pl.BlockSpec((MBLK, MBLK), const2),
            ],
            out_specs=pl.BlockSpec((1, s_len, PAIR), pair),
            out_shape=jax.ShapeDtypeStruct((b, s_len, d), BF16),
            scratch_shapes=[pltpu.VMEM((2 * s_len, PAIR), F32), pltpu.VMEM((2 * s_len, PAIR), F32)],
            compiler_params=pltpu.CompilerParams(
                dimension_semantics=("arbitrary", "arbitrary"), vmem_limit_bytes=VMEM_LIMIT),
            name="attn_sample_rest",
        )(q, cache_k, cache_v, k_meta, v_meta, r_state, acc_state, u256, u128)

    return lax.cond(jnp.min(r_state) < R_DONE, rest, lambda: o_recent)


def _merge_mlp_kernel(x_ref, conv_ref, attn_ref, gates_ref, wout_ref, gmlp_ref, wup_ref, wdown_ref, gfin_ref,
                      y_ref):
    g = gates_ref[...]
    mixed = (g[:, :D_MODEL].astype(F32) * conv_ref[...].astype(F32)
             + g[:, D_MODEL:].astype(F32) * attn_ref[...].astype(F32))
    x1 = x_ref[...] + jnp.dot(mixed.astype(BF16), wout_ref[...], preferred_element_type=F32)
    h = _rms(x1, gmlp_ref[...]).astype(BF16)
    acc = x1
    for c in range(D_FF // D_MODEL):
        cs = slice(c * D_MODEL, (c + 1) * D_MODEL)
        u = jnp.maximum(jnp.dot(h, wup_ref[:, cs], preferred_element_type=F32), 0.0)
        acc = acc + jnp.dot((u * u).astype(BF16), wdown_ref[cs, :], preferred_element_type=F32)
    y_ref[...] = _rms(acc, gfin_ref[...])


def _merge_mlp(x, conv, attn, gates, w_out, g_mlp, w_up, w_down, g_final, tm):
    m, d = x.shape
    row = lambda i: (i, 0)
    const = lambda i: (0, 0)
    resident = functools.partial(pl.BlockSpec, index_map=const, pipeline_mode=pl.Buffered(1))
    return pl.pallas_call(
        _merge_mlp_kernel,
        grid=(m // tm,),
        in_specs=[
            pl.BlockSpec((tm, d), row), pl.BlockSpec((tm, d), row), pl.BlockSpec((tm, d), row),
            pl.BlockSpec((tm, 2 * d), row),
            resident((d, d)), resident((1, d)), resident((d, D_FF)), resident((D_FF, d)), resident((1, d)),
        ],
        out_specs=pl.BlockSpec((tm, d), row),
        out_shape=jax.ShapeDtypeStruct((m, d), F32),
        compiler_params=pltpu.CompilerParams(
            dimension_semantics=("arbitrary",), vmem_limit_bytes=VMEM_LIMIT),
        name="merge_mlp",
    )(x, conv, attn, gates, w_out, g_mlp.reshape(1, d), w_up, w_down, g_final.reshape(1, d))


def kernel(x_prompt, x_sample, cache_k, cache_v, cache_conv, meta, g_mix, w_in, w_dw, b_dw, g_ln_conv,
           b_ln_conv, w_pw2, w_out, g_mlp, w_up, w_down, g_final):
    b, t, d = x_prompt.shape
    sb, s_len, _ = x_sample.shape
    depth, _, past, _, _ = cache_k.shape
    assert depth == 1 and d == D_MODEL and meta.shape == (N_META, d)
    assert t % (Q_SUB * KBLK) == 0 and past % (RECENT_BLOCKS * KBLK) == 0 and s_len % 16 == 0 and CONV_K - 1 <= s_len <= MBLK
    n_s = sb * s_len

    w_in_b = w_in[0].astype(BF16)
    w_pw2_b = w_pw2[0].astype(BF16)
    w_out_b = w_out[0].astype(BF16)
    w_up_b = w_up[0].astype(BF16)
    w_down_b = w_down[0].astype(BF16)

    x_sm = jnp.concatenate([x_sample.reshape(n_s, d), meta.astype(F32)], axis=0)
    glu_s, q_s, k_s, kb_s, v_s, vb_s, gates_s = _in_proj(x_sm, g_mix[0], w_in_b, tm=n_s + N_META)

    pad_meta = ((0, MBLK - N_META), (0, 0))
    kb_meta = jnp.pad(kb_s[n_s:], pad_meta)
    vb_meta = jnp.pad(vb_s[n_s:], pad_meta)

    conv_w = (w_dw[0], b_dw[0], g_ln_conv[0], b_ln_conv[0], w_pw2_b)
    w_main_b = jnp.concatenate([w_in[0, :, :3 * d], w_in[0, :, 5 * d:]], axis=1).astype(BF16)
    w_kvt_b = _transposed_columns(w_in[0], 3 * d, 2 * d)
    head_p = jnp.pad(glu_s[n_s:], ((HALO - N_META, 0), (0, 0)))[None]
    q_p, gates_p, conv_p, kt_p, vt_p, glu_tail_p = _in_proj_prompt(
        x_prompt, meta.astype(F32), head_p, g_mix[0], w_main_b, w_kvt_b, *conv_w, tm=512)

    glu_s3 = glu_s[:n_s].reshape(sb, s_len, d)
    head_s = jnp.pad(cache_conv[0], ((0, 0), (HALO - (CONV_K - 1), 0), (0, 0)))
    conv_s = _conv_branch(glu_s3, head_s, *conv_w)

    attn_p = _attn_prompt(q_p, kt_p, vt_p)
    pad_new = ((0, 0), (0, MBLK - s_len), (0, 0))
    feature_major = lambda c: c.transpose(0, 2, 3, 1).reshape(sb, d, past)
    attn_s = _attn_sample(
        q_s[:n_s].reshape(sb, s_len, d),
        jnp.pad(kb_s[:n_s].reshape(sb, s_len, d), pad_new), jnp.pad(vb_s[:n_s].reshape(sb, s_len, d), pad_new),
        feature_major(cache_k[0]), feature_major(cache_v[0]), kb_meta, vb_meta)

    mlp_w = (w_out_b, g_mlp[0], w_up_b, w_down_b, g_final)
    y_p = _merge_mlp(x_prompt.reshape(b * t, d), conv_p.reshape(b * t, d), attn_p.reshape(b * t, d),
                     gates_p.reshape(b * t, 2 * d), *mlp_w, tm=512)
    y_s = _merge_mlp(x_sample.reshape(n_s, d), conv_s.reshape(n_s, d), attn_s.reshape(n_s, d), gates_s[:n_s],
                     *mlp_w, tm=n_s)

    def key_major(xt):
        return xt.reshape(1, b, N_HEADS, HEAD_DIM, N_META + t).transpose(0, 1, 4, 2, 3)

    return (
        y_p.reshape(b, t, d),
        y_s.reshape(sb, s_len, d),
        key_major(kt_p),
        key_major(vt_p),
        glu_tail_p[:, HALO - (CONV_K - 1):][None],
        k_s[:n_s].reshape(1, sb, s_len, N_HEADS, HEAD_DIM),
        v_s[:n_s].reshape(1, sb, s_len, N_HEADS, HEAD_DIM),
        glu_s3[:, s_len - (CONV_K - 1):][None],
    )
```

```python
import functools

import jax
import jax.numpy as jnp
from jax import lax
from jax.experimental import pallas as pl
from jax.experimental.pallas import tpu as pltpu

F32 = jnp.float32
BF16 = jnp.bfloat16

D_MODEL = 1024
N_META = 16
CONV_K = 31
N_HEADS = 16
HEAD_DIM = 64
D_FF = 4 * D_MODEL
EPS = 1e-6
ATTN_SCALE = HEAD_DIM ** -0.5
LOG2E = 1.4426950408889634
PAIR = 2 * HEAD_DIM
SUBLANES = 8
HALO = 32
KBLK = 256
MBLK = 128
Q_SUB = 8
RECENT_BLOCKS = 2
MASKED = -1e30
R_DONE = 160.0
VMEM_LIMIT = 56 * 1024 * 1024


def _sigmoid(x):
    return 1.0 / (1.0 + jnp.exp(-x))


def _rms(x, g):
    return x * lax.rsqrt(jnp.mean(x * x, axis=-1, keepdims=True) + EPS) * g


def _in_proj_kernel(x_ref, g_ref, w_ref, wb_ref, glu_ref, q_ref, k_ref, kb_ref, v_ref, vb_ref,
                    gates_ref, h_scr):
    j = pl.program_id(1)

    @pl.when(j == 0)
    def _():
        h_scr[...] = _rms(x_ref[...], g_ref[...]).astype(BF16)

    h = h_scr[...]
    p = jnp.dot(h, w_ref[...], preferred_element_type=F32)

    @pl.when(j == 0)
    def _():
        b = jnp.dot(h, wb_ref[...], preferred_element_type=F32)
        glu_ref[...] = p * _sigmoid(b)

    @pl.when(j == 1)
    def _():
        q_ref[...] = (p * (ATTN_SCALE * LOG2E)).astype(BF16)

    @pl.when(j == 2)
    def _():
        k_ref[...] = p
        kb_ref[...] = p.astype(BF16)

    @pl.when(j == 3)
    def _():
        v_ref[...] = p
        vb_ref[...] = p.astype(BF16)

    @pl.when(j == 4)
    def _():
        gates_ref[:, :D_MODEL] = _sigmoid(p).astype(BF16)

    @pl.when(j == 5)
    def _():
        gates_ref[:, D_MODEL:] = _sigmoid(p).astype(BF16)


def _in_proj(x, g, w_in, tm):
    m = x.shape[0]
    d = D_MODEL
    row = lambda i, j: (i, 0)
    return pl.pallas_call(
        _in_proj_kernel,
        grid=(m // tm, 6),
        in_specs=[
            pl.BlockSpec((tm, d), row),
            pl.BlockSpec((1, d), lambda i, j: (0, 0)),
            pl.BlockSpec((d, d), lambda i, j: (0, jnp.where(j == 0, 0, j + 1))),
            pl.BlockSpec((d, d), lambda i, j: (0, 1)),
        ],
        out_specs=[
            pl.BlockSpec((tm, d), row), pl.BlockSpec((tm, d), row),
            pl.BlockSpec((tm, d), row), pl.BlockSpec((tm, d), row),
            pl.BlockSpec((tm, d), row), pl.BlockSpec((tm, d), row),
            pl.BlockSpec((tm, 2 * d), row),
        ],
        out_shape=[
            jax.ShapeDtypeStruct((m, d), F32), jax.ShapeDtypeStruct((m, d), BF16),
            jax.ShapeDtypeStruct((m, d), F32), jax.ShapeDtypeStruct((m, d), BF16),
            jax.ShapeDtypeStruct((m, d), F32), jax.ShapeDtypeStruct((m, d), BF16),
            jax.ShapeDtypeStruct((m, 2 * d), BF16),
        ],
        scratch_shapes=[pltpu.VMEM((tm, d), BF16)],
        compiler_params=pltpu.CompilerParams(
            dimension_semantics=("arbitrary", "arbitrary"), vmem_limit_bytes=VMEM_LIMIT),
        name="in_proj",
    )(x, g.reshape(1, d), w_in, w_in)


def _transpose_kernel(w_ref, o_ref):
    o_ref[...] = w_ref[...].T.astype(BF16)


def _transposed_columns(w, first, count, tn=512):
    k = w.shape[0]
    return pl.pallas_call(
        _transpose_kernel,
        grid=(count // tn,),
        in_specs=[pl.BlockSpec((k, tn), lambda j: (0, first // tn + j))],
        out_specs=pl.BlockSpec((tn, k), lambda j: (j, 0)),
        out_shape=jax.ShapeDtypeStruct((count, k), BF16),
        compiler_params=pltpu.CompilerParams(dimension_semantics=("arbitrary",), vmem_limit_bytes=VMEM_LIMIT),
        name="transpose_weights",
    )(w)


def _depthwise_conv_block(c, ext_scr, sh_scr, y_scr, wdw_ref, bdw_ref, tm):
    rows = HALO + tm - SUBLANES
    rc = min(tm, 64)
    first = HALO - (CONV_K - 1)
    cs = slice(c * 128, (c + 1) * 128)
    sh = sh_scr.at[c % 2]
    for m in range(1, SUBLANES):
        sh[m - 1, 0:rows, :] = ext_scr[m:m + rows, cs]
    for r in range(tm // rc):
        acc = jnp.broadcast_to(bdw_ref[:, cs], (rc, 128))
        for t in range(CONV_K):
            m = (t + first) % SUBLANES
            lo = r * rc + (t + first) - m
            src = ext_scr[lo:lo + rc, cs] if m == 0 else sh[m - 1, lo:lo + rc, :]
            acc = acc + wdw_ref[t:t + 1, cs] * src
        y_scr[r * rc:(r + 1) * rc, cs] = acc


N_CONV_BLOCKS = D_MODEL // 128


def _ln_swish_pointwise(y, gln_ref, bln_ref, wpw_ref):
    yc = y - jnp.mean(y, axis=-1, keepdims=True)
    var = jnp.mean(yc * yc, axis=-1, keepdims=True)
    ln = yc * lax.rsqrt(var + EPS) * gln_ref[...] + bln_ref[...]
    s = ln * _sigmoid(ln)
    return jnp.dot(s.astype(BF16), wpw_ref[...], preferred_element_type=F32)


def _in_proj_prompt_kernel(x_ref, meta_ref, head_ref, g_ref, w_ref, wt_ref, wdw_ref, bdw_ref, gln_ref, bln_ref,
                           wpw_ref, q_ref, gates_ref, conv_ref, kt_ref, vt_ref, glu_tail_ref,
                           h_scr, hs_scr, carry_scr, ext_scr, sh_scr, y_scr, *, tm, n_tiles):
    t = pl.program_id(1)
    d = D_MODEL
    nt = (((1,), (1,)), ((), ()))

    def keys_values():
        hs_scr[0:N_META, :] = carry_scr[...]
        hs_scr[N_META:tm, :] = h_scr[0:tm - N_META, :]
        carry_scr[...] = h_scr[tm - N_META:tm, :]
        hs = hs_scr[...]
        kt_ref[0] = lax.dot_general(wt_ref[0:d, :], hs, nt, preferred_element_type=F32)
        vt_ref[0] = lax.dot_general(wt_ref[d:2 * d, :], hs, nt, preferred_element_type=F32)

    @pl.when(t == 0)
    def _():
        carry_scr[...] = _rms(meta_ref[...], g_ref[...]).astype(BF16)
        ext_scr[0:HALO, :] = head_ref[0]

    @pl.when(jnp.logical_and(t > 0, t < n_tiles))
    def _():
        ext_scr[0:HALO, :] = ext_scr[tm:tm + HALO, :]

    @pl.when(t < n_tiles)
    def _():
        h_scr[...] = _rms(x_ref[0], g_ref[...]).astype(BF16)
        h = h_scr[...]
        proj = lambda c: jnp.dot(h, w_ref[:, c * d:(c + 1) * d], preferred_element_type=F32)
        conv = lambda c: _depthwise_conv_block(c, ext_scr, sh_scr, y_scr, wdw_ref, bdw_ref, tm)
        ext_scr[HALO:HALO + tm, :] = proj(0) * _sigmoid(proj(1))
        glu_tail_ref[0] = ext_scr[tm:tm + HALO, :]
        conv(0)
        conv(1)
        q_ref[0] = (proj(2) * (ATTN_SCALE * LOG2E)).astype(BF16)
        conv(2)
        gates_ref[0, :, :d] = _sigmoid(proj(3)).astype(BF16)
        conv(3)
        gates_ref[0, :, d:] = _sigmoid(proj(4)).astype(BF16)
        conv(4)
        conv(5)
        keys_values()
        conv(6)
        conv(7)
        conv_ref[0] = _ln_swish_pointwise(y_scr[...], gln_ref, bln_ref, wpw_ref).astype(BF16)

    @pl.when(t == n_tiles)
    def _():
        keys_values()


def _in_proj_prompt(x, meta, head, g, w_main, w_kvt, w_dw, b_dw, g_ln, b_ln, w_pw2, tm):
    b, t_len, d = x.shape
    n_tiles = t_len // tm
    row = lambda bi, t: (bi, jnp.minimum(t, n_tiles - 1), 0)
    const = lambda bi, t: (0, 0)
    resident = functools.partial(pl.BlockSpec, index_map=const, pipeline_mode=pl.Buffered(1))
    return pl.pallas_call(
        functools.partial(_in_proj_prompt_kernel, tm=tm, n_tiles=n_tiles),
        grid=(b, n_tiles + 1),
        in_specs=[
            pl.BlockSpec((1, tm, d), row),
            resident((N_META, d)),
            pl.BlockSpec((1, HALO, d), lambda bi, t: (0, 0, 0), pipeline_mode=pl.Buffered(1)),
            resident((1, d)), resident((d, 5 * d)), resident((2 * d, d)),
            resident((CONV_K, d)), resident((1, d)), resident((1, d)), resident((1, d)), resident((d, d)),
        ],
        out_specs=[
            pl.BlockSpec((1, tm, d), row), pl.BlockSpec((1, tm, 2 * d), row), pl.BlockSpec((1, tm, d), row),
            pl.BlockSpec((1, d, tm), lambda bi, t: (bi, 0, t)),
            pl.BlockSpec((1, d, tm), lambda bi, t: (bi, 0, t)),
            pl.BlockSpec((1, HALO, d), lambda bi, t: (bi, 0, 0)),
        ],
        out_shape=[
            jax.ShapeDtypeStruct((b, t_len, d), BF16), jax.ShapeDtypeStruct((b, t_len, 2 * d), BF16),
            jax.ShapeDtypeStruct((b, t_len, d), BF16),
            jax.ShapeDtypeStruct((b, d, N_META + t_len), F32), jax.ShapeDtypeStruct((b, d, N_META + t_len), F32),
            jax.ShapeDtypeStruct((b, HALO, d), F32),
        ],
        scratch_shapes=[pltpu.VMEM((tm, d), BF16), pltpu.VMEM((tm, d), BF16), pltpu.VMEM((N_META, d), BF16),
                        pltpu.VMEM((HALO + tm, d), F32), pltpu.VMEM((2, SUBLANES - 1, HALO + tm, 128), F32),
                        pltpu.VMEM((tm, d), F32)],
        compiler_params=pltpu.CompilerParams(
            dimension_semantics=("arbitrary", "arbitrary"), vmem_limit_bytes=VMEM_LIMIT),
        name="in_proj_prompt",
    )(x, meta, head, g.reshape(1, d), w_main, w_kvt, w_dw, b_dw.reshape(1, d), g_ln.reshape(1, d),
      b_ln.reshape(1, d), w_pw2)


def _conv_kernel(glu_ref, head_ref, wdw_ref, bdw_ref, gln_ref, bln_ref, wpw_ref, out_ref,
                 ext_scr, sh_scr, y_scr, *, tm):
    ext_scr[0:HALO, :] = head_ref[0]
    ext_scr[HALO:HALO + tm, :] = glu_ref[0]
    for c in range(N_CONV_BLOCKS):
        _depthwise_conv_block(c, ext_scr, sh_scr, y_scr, wdw_ref, bdw_ref, tm)
    out_ref[0] = _ln_swish_pointwise(y_scr[...], gln_ref, bln_ref, wpw_ref).astype(BF16)


def _conv_branch(glu, head, w_dw, b_dw, g_ln, b_ln, w_pw2):
    b, tm, d = glu.shape
    const = lambda bi: (0, 0)
    seq = lambda bi: (bi, 0, 0)
    return pl.pallas_call(
        functools.partial(_conv_kernel, tm=tm),
        grid=(b,),
        in_specs=[
            pl.BlockSpec((1, tm, d), seq), pl.BlockSpec((1, HALO, d), seq),
            pl.BlockSpec((CONV_K, d), const),
            pl.BlockSpec((1, d), const), pl.BlockSpec((1, d), const), pl.BlockSpec((1, d), const),
            pl.BlockSpec((d, d), const),
        ],
        out_specs=pl.BlockSpec((1, tm, d), seq),
        out_shape=jax.ShapeDtypeStruct((b, tm, d), BF16),
        scratch_shapes=[pltpu.VMEM((HALO + tm, d), F32), pltpu.VMEM((2, SUBLANES - 1, HALO + tm, 128), F32),
                        pltpu.VMEM((tm, d), F32)],
        compiler_params=pltpu.CompilerParams(dimension_semantics=("arbitrary",), vmem_limit_bytes=VMEM_LIMIT),
        name="conv_branch",
    )(glu, head, w_dw, b_dw.reshape(1, d), g_ln.reshape(1, d), b_ln.reshape(1, d), w_pw2)


def _stack_heads(q):
    lane = lax.broadcasted_iota(jnp.int32, q.shape, 1)
    zero = jnp.zeros_like(q)
    return jnp.concatenate([jnp.where(lane < HEAD_DIM, q, zero), jnp.where(lane >= HEAD_DIM, q, zero)], axis=0)


def _unstack_heads(acc):
    r = acc.shape[0] // 2
    lane = lax.broadcasted_iota(jnp.int32, (r, PAIR), 1)
    return jnp.where(lane < HEAD_DIM, acc[:r], acc[r:])


def _sb_scores(z, mask):
    sp = jnp.maximum(z, 0.0) + jnp.log2(1.0 + jnp.exp2(-jnp.abs(z)))
    ls = z - sp
    if mask is not None:
        sp = jnp.where(mask, sp, 0.0)
        ls = jnp.where(mask, ls, MASKED)
    return ls, sp.astype(BF16)


def _sb_weights(ls, spb, u, r):
    cs = jnp.dot(spb, u, preferred_element_type=F32)
    x = ls - cs
    total = cs[:, 0:1] + spb[:, 0:1].astype(F32)
    mass = jnp.broadcast_to(total, (ls.shape[0], PAIR))
    if r is not None:
        x = x - (r if ls.shape[1] == PAIR else jnp.concatenate([r] * (ls.shape[1] // PAIR), axis=1))
        mass = mass + r
    return jnp.exp2(x).astype(BF16), mass


def _sb_first(z_d, v_d, mask_d, u_d, z_p=None, v_p=None, u_p=None):
    ls_d, spb_d = _sb_scores(z_d, mask_d)
    if z_p is not None:
        ls_p, spb_p = _sb_scores(z_p, None)
    a_d, r = _sb_weights(ls_d, spb_d, u_d, None)
    acc = jnp.dot(a_d, v_d, preferred_element_type=F32)
    if z_p is not None:
        a_p, r = _sb_weights(ls_p, spb_p, u_p, r)
        acc = acc + jnp.dot(a_p, v_p, preferred_element_type=F32)
    return r, acc


def _sb_block(z, v, u, mask, r_ref, acc_ref):
    ls, spb = _sb_scores(z, mask)
    a, r = _sb_weights(ls, spb, u, r_ref[...])
    acc_ref[...] += jnp.dot(a, v, preferred_element_type=F32)
    r_ref[...] = r
    return jnp.min(r)


def _causal_mask(rows_per_head, width):
    row = lax.broadcasted_iota(jnp.int32, (2 * rows_per_head, width), 0)
    col = lax.broadcasted_iota(jnp.int32, (2 * rows_per_head, width), 1)
    return col < jnp.where(row >= rows_per_head, row - rows_per_head, row)


def _meta_mask(rows, width):
    return lax.broadcasted_iota(jnp.int32, (rows, width), 1) < N_META


def _attn_prompt_kernel(q_ref, kt_ref, vt_ref, u256_ref, u128_ref, o_ref,
                        kt_scr, v_scr, kmt_scr, vm_scr, r_scr, acc_scr):
    g = pl.program_id(2)
    tq = KBLK

    @pl.when(g == 0)
    def _():
        for c in range(v_scr.shape[0] // KBLK):
            cols = slice(N_META + c * KBLK, N_META + (c + 1) * KBLK)
            kt_scr[:, c * KBLK:(c + 1) * KBLK] = kt_ref[0, :, cols].astype(BF16)
            v_scr[c * KBLK:(c + 1) * KBLK, :] = vt_ref[0, :, cols].T.astype(BF16)
        kmt_scr[...] = kt_ref[0, :, 0:MBLK].astype(BF16)
        vm_scr[...] = vt_ref[0, :, 0:MBLK].T.astype(BF16)

    u256 = u256_ref[...]
    n_sub = q_ref.shape[1] // tq
    qs = [_stack_heads(q_ref[0, s * tq:(s + 1) * tq, :]) for s in range(n_sub)]

    def scores(s, j):
        return jnp.dot(qs[s], kt_scr[:, pl.ds(pl.multiple_of(j * tq, tq), tq)], preferred_element_type=F32)

    def values(j):
        return v_scr[pl.ds(pl.multiple_of(j * tq, tq), tq), :]

    def first_blocks(first_has_past):
        mask = _causal_mask(tq, tq)
        for s in range(n_sub):
            qi = g * n_sub + s
            if s > 0 or first_has_past:
                r, acc = _sb_first(scores(s, qi), values(qi), mask, u256, scores(s, qi - 1), values(qi - 1), u256)
            else:
                r, acc = _sb_first(scores(s, qi), values(qi), mask, u256)
            r_scr[s] = r
            acc_scr[s] = acc

    @pl.when(g == 0)
    def _():
        first_blocks(False)

    @pl.when(g > 0)
    def _():
        first_blocks(True)

    r_all = r_scr[0]
    for s in range(1, n_sub):
        r_all = jnp.minimum(r_all, r_scr[s])

    @pl.when(jnp.min(r_all) < R_DONE)
    def _():
        for s in range(n_sub):
            qi = g * n_sub + s
            r_ref, acc_ref = r_scr.at[s], acc_scr.at[s]

            def cond(c, qi=qi):
                return jnp.logical_and(c[0] < qi - 1, c[1] < R_DONE)

            def body(c, s=s, qi=qi, r_ref=r_ref, acc_ref=acc_ref):
                j = qi - 2 - c[0]
                return c[0] + 1, _sb_block(scores(s, j), values(j), u256, None, r_ref, acc_ref)

            _, rmin = lax.while_loop(cond, body, (jnp.int32(0), jnp.min(r_ref[...])))

            @pl.when(rmin < R_DONE)
            def _(s=s, r_ref=r_ref, acc_ref=acc_ref):
                z = jnp.dot(qs[s], kmt_scr[...], preferred_element_type=F32)
                _sb_block(z, vm_scr[...], u128_ref[...], _meta_mask(2 * tq, MBLK), r_ref, acc_ref)

    for s in range(n_sub):
        o_ref[0, s * tq:(s + 1) * tq, :] = _unstack_heads(acc_scr[s]).astype(BF16)


def _suffix_sum_matrix(w):
    j = lax.broadcasted_iota(jnp.int32, (w, w), 0)
    s = lax.broadcasted_iota(jnp.int32, (w, w), 1)
    return (j > s).astype(BF16)


def _attn_prompt(q, kt, vt):
    b, t, d = q.shape
    tq = Q_SUB * KBLK
    const2 = lambda bi, hp, g: (0, 0)
    kv_spec = pl.BlockSpec((1, PAIR, N_META + t), lambda bi, hp, g: (bi, hp, 0))
    return pl.pallas_call(
        _attn_prompt_kernel,
        grid=(b, d // PAIR, t // tq),
        in_specs=[
            pl.BlockSpec((1, tq, PAIR), lambda bi, hp, g: (bi, g, hp)),
            kv_spec, kv_spec,
            pl.BlockSpec((KBLK, KBLK), const2),
            pl.BlockSpec((MBLK, MBLK), const2),
        ],
        out_specs=pl.BlockSpec((1, tq, PAIR), lambda bi, hp, g: (bi, g, hp)),
        out_shape=jax.ShapeDtypeStruct((b, t, d), BF16),
        scratch_shapes=[pltpu.VMEM((PAIR, t), BF16), pltpu.VMEM((t, PAIR), BF16),
                        pltpu.VMEM((PAIR, MBLK), BF16), pltpu.VMEM((MBLK, PAIR), BF16),
                        pltpu.VMEM((Q_SUB, 2 * KBLK, PAIR), F32), pltpu.VMEM((Q_SUB, 2 * KBLK, PAIR), F32)],
        compiler_params=pltpu.CompilerParams(
            dimension_semantics=("arbitrary", "arbitrary", "arbitrary"), vmem_limit_bytes=VMEM_LIMIT),
        name="attn_prompt",
    )(q, kt, vt, _suffix_sum_matrix(KBLK), _suffix_sum_matrix(MBLK))


def _attn_sample_kernel(q_ref, kn_ref, vn_ref, ck_ref, cv_ref, u256_ref, u128_ref, o_ref, r_ref, acc_ref,
                        *, s_len, n_pairs):
    u256 = u256_ref[...]
    u128 = u128_ref[...]
    nt = (((1,), (1,)), ((), ()))
    lanes = [slice(p * PAIR, (p + 1) * PAIR) for p in range(n_pairs)]
    qs = [_stack_heads(q_ref[0, :, ln]) for ln in lanes]

    def cache_block(p, j):
        cols = slice(j * KBLK, (j + 1) * KBLK)
        z = jnp.dot(qs[p], ck_ref[0, lanes[p], cols].astype(BF16), preferred_element_type=F32)
        return z, cv_ref[0, lanes[p], cols].T.astype(BF16)

    mask = _causal_mask(s_len, MBLK)
    r_all = None
    for p in range(n_pairs):
        z_new = lax.dot_general(qs[p], kn_ref[0, :, lanes[p]], nt, preferred_element_type=F32)
        z_past, v_past = cache_block(p, RECENT_BLOCKS - 1)
        r, acc = _sb_first(z_new, vn_ref[0, :, lanes[p]], mask, u128, z_past, v_past, u256)
        r_ref[0, p] = r
        acc_ref[0, p] = acc
        r_all = r if r_all is None else jnp.minimum(r_all, r)

    @pl.when(jnp.min(r_all) < R_DONE)
    def _():
        for p in range(n_pairs):
            for j in range(RECENT_BLOCKS - 2, -1, -1):
                z, v = cache_block(p, j)
                _sb_block(z, v, u256, None, r_ref.at[0, p], acc_ref.at[0, p])

    for p in range(n_pairs):
        o_ref[0, :, lanes[p]] = _unstack_heads(acc_ref[0, p]).astype(BF16)


def _attn_sample_rest_kernel(q_ref, ck_ref, cv_ref, km_ref, vm_ref, r_in_ref, acc_in_ref, u256_ref, u128_ref,
                             o_ref, r_scr, acc_scr, *, s_len, n_rest):
    qs = _stack_heads(q_ref[0])
    u256 = u256_ref[...]
    nt = (((1,), (1,)), ((), ()))
    r_scr[...] = r_in_ref[0, 0]
    acc_scr[...] = acc_in_ref[0, 0]

    def cond(c):
        return jnp.logical_and(c[0] < n_rest, c[1] < R_DONE)

    def body(c):
        start = pl.multiple_of((n_rest - 1 - c[0]) * KBLK, KBLK)
        z = jnp.dot(qs, ck_ref[0, :, pl.ds(start, KBLK)].astype(BF16), preferred_element_type=F32)
        v = cv_ref[0, :, pl.ds(start, KBLK)].T.astype(BF16)
        return c[0] + 1, _sb_block(z, v, u256, None, r_scr, acc_scr)

    _, rmin = lax.while_loop(cond, body, (jnp.int32(0), jnp.min(r_scr[...])))

    @pl.when(rmin < R_DONE)
    def _():
        z = lax.dot_general(qs, km_ref[...], nt, preferred_element_type=F32)
        _sb_block(z, vm_ref[...], u128_ref[...], _meta_mask(2 * s_len, MBLK), r_scr, acc_scr)

    o_ref[0] = _unstack_heads(acc_scr[...]).astype(BF16)


def _attn_sample(q, k_new, v_new, cache_k, cache_v, k_meta, v_meta):
    b, s_len, d = q.shape
    p = cache_k.shape[2]
    n_pairs = 4
    n_groups = d // (n_pairs * PAIR)
    recent = RECENT_BLOCKS * KBLK
    grp = lambda bi, h: (bi, 0, h)
    const2 = lambda bi, h: (0, 0)
    state = jax.ShapeDtypeStruct((b, d // PAIR, 2 * s_len, PAIR), F32)
    state_spec = pl.BlockSpec((1, n_pairs, 2 * s_len, PAIR), lambda bi, h: (bi, h, 0, 0))
    recent_spec = pl.BlockSpec((1, n_pairs * PAIR, recent), lambda bi, h: (bi, h, p // recent - 1))
    u256, u128 = _suffix_sum_matrix(KBLK), _suffix_sum_matrix(MBLK)
    o_recent, r_state, acc_state = pl.pallas_call(
        functools.partial(_attn_sample_kernel, s_len=s_len, n_pairs=n_pairs),
        grid=(b, n_groups),
        in_specs=[
            pl.BlockSpec((1, s_len, n_pairs * PAIR), grp),
            pl.BlockSpec((1, MBLK, n_pairs * PAIR), grp), pl.BlockSpec((1, MBLK, n_pairs * PAIR), grp),
            recent_spec, recent_spec,
            pl.BlockSpec((KBLK, KBLK), const2), pl.BlockSpec((MBLK, MBLK), const2),
        ],
        out_specs=[pl.BlockSpec((1, s_len, n_pairs * PAIR), grp), state_spec, state_spec],
        out_shape=[jax.ShapeDtypeStruct((b, s_len, d), BF16), state, state],
        compiler_params=pltpu.CompilerParams(
            dimension_semantics=("arbitrary", "arbitrary"), vmem_limit_bytes=VMEM_LIMIT),
        name="attn_sample",
    )(q, k_new, v_new, cache_k, cache_v, u256, u128)

    def rest():
        pair = lambda bi, hp: (bi, 0, hp)
        one_state = pl.BlockSpec((1, 1, 2 * s_len, PAIR), lambda bi, hp: (bi, hp, 0, 0))
        cache_spec = pl.BlockSpec((1, PAIR, p), lambda bi, hp: (bi, hp, 0))
        return pl.pallas_call(
            functools.partial(_attn_sample_rest_kernel, s_len=s_len, n_rest=p // KBLK - RECENT_BLOCKS),
            grid=(b, d // PAIR),
            in_specs=[
                pl.BlockSpec((1, s_len, PAIR), pair),
                cache_spec, cache_spec,
                pl.BlockSpec((MBLK, PAIR), lambda bi, hp: (0, hp)),
                pl.BlockSpec((MBLK, PAIR), lambda bi, hp: (0, hp)),
                one_state, one_state,
                pl.BlockSpec((KBLK, KBLK), const2), pl.BlockSpec((MBLK, MBLK), const2),
            ],
            out_specs=pl.BlockSpec((1, s_len, PAIR), pair),
            out_shape=jax.ShapeDtypeStruct((b, s_len, d), BF16),
            scratch_shapes=[pltpu.VMEM((2 * s_len, PAIR), F32), pltpu.VMEM((2 * s_len, PAIR), F32)],
            compiler_params=pltpu.CompilerParams(
                dimension_semantics=("arbitrary", "arbitrary"), vmem_limit_bytes=VMEM_LIMIT),
            name="attn_sample_rest",
        )(q, cache_k, cache_v, k_meta, v_meta, r_state, acc_state, u256, u128)

    return lax.cond(jnp.min(r_state) < R_DONE, rest, lambda: o_recent)


def _merge_mlp_kernel(x_ref, conv_ref, attn_ref, gates_ref, wout_ref, gmlp_ref, wup_ref, wdown_ref, gfin_ref,
                      y_ref):
    g = gates_ref[...]
    mixed = (g[:, :D_MODEL].astype(F32) * conv_ref[...].astype(F32)
             + g[:, D_MODEL:].astype(F32) * attn_ref[...].astype(F32))
    x1 = x_ref[...] + jnp.dot(mixed.astype(BF16), wout_ref[...], preferred_element_type=F32)
    h = _rms(x1, gmlp_ref[...]).astype(BF16)
    acc = x1
    for c in range(D_FF // D_MODEL):
        cs = slice(c * D_MODEL, (c + 1) * D_MODEL)
        u = jnp.maximum(jnp.dot(h, wup_ref[:, cs], preferred_element_type=F32), 0.0)
        acc = acc + jnp.dot((u * u).astype(BF16), wdown_ref[cs, :], preferred_element_type=F32)
    y_ref[...] = _rms(acc, gfin_ref[...])


def _merge_mlp(x, conv, attn, gates, w_out, g_mlp, w_up, w_down, g_final, tm):
    m, d = x.shape
    row = lambda i: (i, 0)
    const = lambda i: (0, 0)
    resident = functools.partial(pl.BlockSpec, index_map=const, pipeline_mode=pl.Buffered(1))
    return pl.pallas_call(
        _merge_mlp_kernel,
        grid=(m // tm,),
        in_specs=[
            pl.BlockSpec((tm, d), row), pl.BlockSpec((tm, d), row), pl.BlockSpec((tm, d), row),
            pl.BlockSpec((tm, 2 * d), row),
            resident((d, d)), resident((1, d)), resident((d, D_FF)), resident((D_FF, d)), resident((1, d)),
        ],
        out_specs=pl.BlockSpec((tm, d), row),
        out_shape=jax.ShapeDtypeStruct((m, d), F32),
        compiler_params=pltpu.CompilerParams(
            dimension_semantics=("arbitrary",), vmem_limit_bytes=VMEM_LIMIT),
        name="merge_mlp",
    )(x, conv, attn, gates, w_out, g_mlp.reshape(1, d), w_up, w_down, g_final.reshape(1, d))


def kernel(x_prompt, x_sample, cache_k, cache_v, cache_conv, meta, g_mix, w_in, w_dw, b_dw, g_ln_conv,
           b_ln_conv, w_pw2, w_out, g_mlp, w_up, w_down, g_final):
    b, t, d = x_prompt.shape
    sb, s_len, _ = x_sample.shape
    depth, _, past, _, _ = cache_k.shape
    assert depth == 1 and d == D_MODEL and meta.shape == (N_META, d)
    assert t % (Q_SUB * KBLK) == 0 and past % (RECENT_BLOCKS * KBLK) == 0 and s_len % 16 == 0 and CONV_K - 1 <= s_len <= MBLK
    n_s = sb * s_len

    w_in_b = w_in[0].astype(BF16)
    w_pw2_b = w_pw2[0].astype(BF16)
    w_out_b = w_out[0].astype(BF16)
    w_up_b = w_up[0].astype(BF16)
    w_down_b = w_down[0].astype(BF16)

    x_sm = jnp.concatenate([x_sample.reshape(n_s, d), meta.astype(F32)], axis=0)
    glu_s, q_s, k_s, kb_s, v_s, vb_s, gates_s = _in_proj(x_sm, g_mix[0], w_in_b, tm=n_s + N_META)

    pad_meta = ((0, MBLK - N_META), (0, 0))
    kb_meta = jnp.pad(kb_s[n_s:], pad_meta)
    vb_meta = jnp.pad(vb_s[n_s:], pad_meta)

    conv_w = (w_dw[0], b_dw[0], g_ln_conv[0], b_ln_conv[0], w_pw2_b)
    w_main_b = jnp.concatenate([w_in_b[:, :3 * d], w_in_b[:, 5 * d:]], axis=1)
    w_kvt_b = _transposed_columns(w_in[0], 3 * d, 2 * d)
    head_p = jnp.pad(glu_s[n_s:], ((HALO - N_META, 0), (0, 0)))[None]
    q_p, gates_p, conv_p, kt_p, vt_p, glu_tail_p = _in_proj_prompt(
        x_prompt, meta.astype(F32), head_p, g_mix[0], w_main_b, w_kvt_b, *conv_w, tm=512)

    glu_s3 = glu_s[:n_s].reshape(sb, s_len, d)
    head_s = jnp.pad(cache_conv[0], ((0, 0), (HALO - (CONV_K - 1), 0), (0, 0)))
    conv_s = _conv_branch(glu_s3, head_s, *conv_w)

    attn_p = _attn_prompt(q_p, kt_p, vt_p)
    pad_new = ((0, 0), (0, MBLK - s_len), (0, 0))
    feature_major = lambda c: c.transpose(0, 2, 3, 1).reshape(sb, d, past)
    attn_s = _attn_sample(
        q_s[:n_s].reshape(sb, s_len, d),
        jnp.pad(kb_s[:n_s].reshape(sb, s_len, d), pad_new), jnp.pad(vb_s[:n_s].reshape(sb, s_len, d), pad_new),
        feature_major(cache_k[0]), feature_major(cache_v[0]), kb_meta, vb_meta)

    mlp_w = (w_out_b, g_mlp[0], w_up_b, w_down_b, g_final)
    y_p = _merge_mlp(x_prompt.reshape(b * t, d), conv_p.reshape(b * t, d), attn_p.reshape(b * t, d),
                     gates_p.reshape(b * t, 2 * d), *mlp_w, tm=512)
    y_s = _merge_mlp(x_sample.reshape(n_s, d), conv_s.reshape(n_s, d), attn_s.reshape(n_s, d), gates_s[:n_s],
                     *mlp_w, tm=n_s)

    def key_major(xt):
        return xt.reshape(1, b, N_HEADS, HEAD_DIM, N_META + t).transpose(0, 1, 4, 2, 3)

    return (
        y_p.reshape(b, t, d),
        y_s.reshape(sb, s_len, d),
        key_major(kt_p),
        key_major(vt_p),
        glu_tail_p[:, HALO - (CONV_K - 1):][None],
        k_s[:n_s].reshape(1, sb, s_len, N_HEADS, HEAD_DIM),
        v_s[:n_s].reshape(1, sb, s_len, N_HEADS, HEAD_DIM),
        glu_s3[:, s_len - (CONV_K - 1):][None],
    )
```

```python
import functools

import jax
import jax.numpy as jnp
from jax import lax
from jax.experimental import pallas as pl
from jax.experimental.pallas import tpu as pltpu

F32 = jnp.float32
BF16 = jnp.bfloat16

D_MODEL = 1024
N_META = 16
CONV_K = 31
N_HEADS = 16
HEAD_DIM = 64
D_FF = 4 * D_MODEL
EPS = 1e-6
ATTN_SCALE = HEAD_DIM ** -0.5
LOG2E = 1.4426950408889634
LANES = 128
SUBLANES = 8
PAIR = 2 * HEAD_DIM
TM = 512
HALO = 32
KBLK = 256
MBLK = 128
Q_SUB = 8
RECENT_BLOCKS = 2
MASKED = -1e30
R_DONE = 160.0
VMEM_LIMIT = 56 * 1024 * 1024


def _sigmoid(x):
    return 1.0 / (1.0 + jnp.exp(-x))


def _rms(x, g):
    return x * lax.rsqrt(jnp.mean(x * x, axis=-1, keepdims=True) + EPS) * g


def _in_proj_kernel(x_ref, g_ref, w_ref, wb_ref, glu_ref, q_ref, k_ref, kb_ref, v_ref, vb_ref,
                    gates_ref, h_scr):
    j = pl.program_id(1)

    @pl.when(j == 0)
    def _():
        h_scr[...] = _rms(x_ref[...], g_ref[...]).astype(BF16)

    h = h_scr[...]
    p = jnp.dot(h, w_ref[...], preferred_element_type=F32)

    @pl.when(j == 0)
    def _():
        b = jnp.dot(h, wb_ref[...], preferred_element_type=F32)
        glu_ref[...] = p * _sigmoid(b)

    @pl.when(j == 1)
    def _():
        q_ref[...] = (p * (ATTN_SCALE * LOG2E)).astype(BF16)

    @pl.when(j == 2)
    def _():
        k_ref[...] = p
        kb_ref[...] = p.astype(BF16)

    @pl.when(j == 3)
    def _():
        v_ref[...] = p
        vb_ref[...] = p.astype(BF16)

    @pl.when(j == 4)
    def _():
        gates_ref[:, :D_MODEL] = _sigmoid(p).astype(BF16)

    @pl.when(j == 5)
    def _():
        gates_ref[:, D_MODEL:] = _sigmoid(p).astype(BF16)


def _in_proj(x, g, w_in, tm):
    m = x.shape[0]
    d = D_MODEL
    row = lambda i, j: (i, 0)
    return pl.pallas_call(
        _in_proj_kernel,
        grid=(m // tm, 6),
        in_specs=[
            pl.BlockSpec((tm, d), row),
            pl.BlockSpec((1, d), lambda i, j: (0, 0)),
            pl.BlockSpec((d, d), lambda i, j: (0, jnp.where(j == 0, 0, j + 1))),
            pl.BlockSpec((d, d), lambda i, j: (0, 1)),
        ],
        out_specs=[
            pl.BlockSpec((tm, d), row), pl.BlockSpec((tm, d), row),
            pl.BlockSpec((tm, d), row), pl.BlockSpec((tm, d), row),
            pl.BlockSpec((tm, d), row), pl.BlockSpec((tm, d), row),
            pl.BlockSpec((tm, 2 * d), row),
        ],
        out_shape=[
            jax.ShapeDtypeStruct((m, d), F32), jax.ShapeDtypeStruct((m, d), BF16),
            jax.ShapeDtypeStruct((m, d), F32), jax.ShapeDtypeStruct((m, d), BF16),
            jax.ShapeDtypeStruct((m, d), F32), jax.ShapeDtypeStruct((m, d), BF16),
            jax.ShapeDtypeStruct((m, 2 * d), BF16),
        ],
        scratch_shapes=[pltpu.VMEM((tm, d), BF16)],
        compiler_params=pltpu.CompilerParams(
            dimension_semantics=("arbitrary", "arbitrary"), vmem_limit_bytes=VMEM_LIMIT),
        name="in_proj",
    )(x, g.reshape(1, d), w_in, w_in)


def _transpose_kernel(w_ref, o_ref):
    o_ref[...] = w_ref[...].T.astype(BF16)


def _transposed_columns(w, first, count, tn=512):
    k = w.shape[0]
    return pl.pallas_call(
        _transpose_kernel,
        grid=(count // tn,),
        in_specs=[pl.BlockSpec((k, tn), lambda j: (0, first // tn + j))],
        out_specs=pl.BlockSpec((tn, k), lambda j: (j, 0)),
        out_shape=jax.ShapeDtypeStruct((count, k), BF16),
        compiler_params=pltpu.CompilerParams(dimension_semantics=("arbitrary",), vmem_limit_bytes=VMEM_LIMIT),
        name="transpose_weights",
    )(w)


def _depthwise_conv_block(c, ext_scr, sh_scr, y_scr, wdw_ref, bdw_ref, tm):
    rows = HALO + tm - SUBLANES
    rc = min(tm, 64)
    first = HALO - (CONV_K - 1)
    cs = slice(c * LANES, (c + 1) * LANES)
    sh = sh_scr.at[c % 2]
    for m in range(1, SUBLANES):
        sh[m - 1, 0:rows, :] = ext_scr[m:m + rows, cs]
    for r in range(tm // rc):
        acc = jnp.broadcast_to(bdw_ref[:, cs], (rc, LANES))
        for t in range(CONV_K):
            m = (t + first) % SUBLANES
            lo = r * rc + (t + first) - m
            src = ext_scr[lo:lo + rc, cs] if m == 0 else sh[m - 1, lo:lo + rc, :]
            acc = acc + wdw_ref[t:t + 1, cs] * src
        y_scr[r * rc:(r + 1) * rc, cs] = acc


N_CONV_BLOCKS = D_MODEL // LANES


def _ln_swish_pointwise(y, gln_ref, bln_ref, wpw_ref):
    yc = y - jnp.mean(y, axis=-1, keepdims=True)
    var = jnp.mean(yc * yc, axis=-1, keepdims=True)
    ln = yc * lax.rsqrt(var + EPS) * gln_ref[...] + bln_ref[...]
    s = ln * _sigmoid(ln)
    return jnp.dot(s.astype(BF16), wpw_ref[...], preferred_element_type=F32)


def _in_proj_prompt_kernel(x_ref, meta_ref, head_ref, g_ref, w_ref, wt_ref, wdw_ref, bdw_ref, gln_ref, bln_ref,
                           wpw_ref, q_ref, gates_ref, conv_ref, kt_ref, vt_ref, glu_tail_ref,
                           h_scr, hs_scr, carry_scr, ext_scr, sh_scr, y_scr, *, tm, n_tiles):
    t = pl.program_id(1)
    d = D_MODEL
    nt = (((1,), (1,)), ((), ()))

    def keys_values():
        hs_scr[0:N_META, :] = carry_scr[...]
        hs_scr[N_META:tm, :] = h_scr[0:tm - N_META, :]
        carry_scr[...] = h_scr[tm - N_META:tm, :]
        hs = hs_scr[...]
        kt_ref[0] = lax.dot_general(wt_ref[0:d, :], hs, nt, preferred_element_type=F32)
        vt_ref[0] = lax.dot_general(wt_ref[d:2 * d, :], hs, nt, preferred_element_type=F32)

    @pl.when(t == 0)
    def _():
        carry_scr[...] = _rms(meta_ref[...], g_ref[...]).astype(BF16)
        ext_scr[0:HALO, :] = head_ref[0]

    @pl.when(jnp.logical_and(t > 0, t < n_tiles))
    def _():
        ext_scr[0:HALO, :] = ext_scr[tm:tm + HALO, :]

    @pl.when(t < n_tiles)
    def _():
        h_scr[...] = _rms(x_ref[0], g_ref[...]).astype(BF16)
        h = h_scr[...]
        proj = lambda c: jnp.dot(h, w_ref[:, c * d:(c + 1) * d], preferred_element_type=F32)
        conv = lambda c: _depthwise_conv_block(c, ext_scr, sh_scr, y_scr, wdw_ref, bdw_ref, tm)
        ext_scr[HALO:HALO + tm, :] = proj(0) * _sigmoid(proj(1))
        glu_tail_ref[0] = ext_scr[tm:tm + HALO, :]
        for c in range(N_CONV_BLOCKS):
            conv(c)
        q_ref[0] = (proj(2) * (ATTN_SCALE * LOG2E)).astype(BF16)
        gates_ref[0, :, :d] = _sigmoid(proj(3)).astype(BF16)
        gates_ref[0, :, d:] = _sigmoid(proj(4)).astype(BF16)
        keys_values()
        conv_ref[0] = _ln_swish_pointwise(y_scr[...], gln_ref, bln_ref, wpw_ref).astype(BF16)

    @pl.when(t == n_tiles)
    def _():
        keys_values()


def _in_proj_prompt(x, meta, head, g, w_main, w_kvt, w_dw, b_dw, g_ln, b_ln, w_pw2, tm):
    b, t_len, d = x.shape
    n_tiles = t_len // tm
    row = lambda bi, t: (bi, jnp.minimum(t, n_tiles - 1), 0)
    const = lambda bi, t: (0, 0)
    resident = functools.partial(pl.BlockSpec, index_map=const, pipeline_mode=pl.Buffered(1))
    return pl.pallas_call(
        functools.partial(_in_proj_prompt_kernel, tm=tm, n_tiles=n_tiles),
        grid=(b, n_tiles + 1),
        in_specs=[
            pl.BlockSpec((1, tm, d), row),
            resident((N_META, d)),
            pl.BlockSpec((1, HALO, d), lambda bi, t: (0, 0, 0), pipeline_mode=pl.Buffered(1)),
            resident((1, d)), resident((d, 5 * d)), resident((2 * d, d)),
            resident((CONV_K, d)), resident((1, d)), resident((1, d)), resident((1, d)), resident((d, d)),
        ],
        out_specs=[
            pl.BlockSpec((1, tm, d), row), pl.BlockSpec((1, tm, 2 * d), row), pl.BlockSpec((1, tm, d), row),
            pl.BlockSpec((1, d, tm), lambda bi, t: (bi, 0, t)),
            pl.BlockSpec((1, d, tm), lambda bi, t: (bi, 0, t)),
            pl.BlockSpec((1, HALO, d), lambda bi, t: (bi, 0, 0)),
        ],
        out_shape=[
            jax.ShapeDtypeStruct((b, t_len, d), BF16), jax.ShapeDtypeStruct((b, t_len, 2 * d), BF16),
            jax.ShapeDtypeStruct((b, t_len, d), BF16),
            jax.ShapeDtypeStruct((b, d, N_META + t_len), F32), jax.ShapeDtypeStruct((b, d, N_META + t_len), F32),
            jax.ShapeDtypeStruct((b, HALO, d), F32),
        ],
        scratch_shapes=[pltpu.VMEM((tm, d), BF16), pltpu.VMEM((tm, d), BF16), pltpu.VMEM((N_META, d), BF16),
                        pltpu.VMEM((HALO + tm, d), F32), pltpu.VMEM((2, SUBLANES - 1, HALO + tm, LANES), F32),
                        pltpu.VMEM((tm, d), F32)],
        compiler_params=pltpu.CompilerParams(
            dimension_semantics=("arbitrary", "arbitrary"), vmem_limit_bytes=VMEM_LIMIT),
        name="in_proj_prompt",
    )(x, meta, head, g.reshape(1, d), w_main, w_kvt, w_dw, b_dw.reshape(1, d), g_ln.reshape(1, d),
      b_ln.reshape(1, d), w_pw2)


def _conv_kernel(glu_ref, head_ref, wdw_ref, bdw_ref, gln_ref, bln_ref, wpw_ref, out_ref,
                 ext_scr, sh_scr, y_scr, *, tm):
    for i in range(glu_ref.shape[0]):
        ext_scr[i, 0:HALO, :] = head_ref[i]
        ext_scr[i, HALO:HALO + tm, :] = glu_ref[i]
        for c in range(N_CONV_BLOCKS):
            _depthwise_conv_block(c, ext_scr.at[i], sh_scr, y_scr.at[pl.ds(i * tm, tm)], wdw_ref, bdw_ref, tm)
    out_ref[...] = _ln_swish_pointwise(y_scr[...], gln_ref, bln_ref, wpw_ref).astype(BF16)


def _conv_branch(glu, head, w_dw, b_dw, g_ln, b_ln, w_pw2):
    b, tm, d = glu.shape
    const2 = lambda i: (0, 0)
    const3 = lambda i: (0, 0, 0)
    return pl.pallas_call(
        functools.partial(_conv_kernel, tm=tm),
        grid=(1,),
        in_specs=[
            pl.BlockSpec((b, tm, d), const3), pl.BlockSpec((b, HALO, d), const3),
            pl.BlockSpec((CONV_K, d), const2),
            pl.BlockSpec((1, d), const2), pl.BlockSpec((1, d), const2), pl.BlockSpec((1, d), const2),
            pl.BlockSpec((d, d), const2),
        ],
        out_specs=pl.BlockSpec((b * tm, d), const2),
        out_shape=jax.ShapeDtypeStruct((b * tm, d), BF16),
        scratch_shapes=[pltpu.VMEM((b, HALO + tm, d), F32), pltpu.VMEM((2, SUBLANES - 1, HALO + tm, LANES), F32),
                        pltpu.VMEM((b * tm, d), F32)],
        compiler_params=pltpu.CompilerParams(dimension_semantics=("arbitrary",), vmem_limit_bytes=VMEM_LIMIT),
        name="conv_branch",
    )(glu, head, w_dw, b_dw.reshape(1, d), g_ln.reshape(1, d), b_ln.reshape(1, d), w_pw2)


def _stack_heads(q):
    lane = lax.broadcasted_iota(jnp.int32, q.shape, 1)
    zero = jnp.zeros_like(q)
    return jnp.concatenate([jnp.where(lane < HEAD_DIM, q, zero), jnp.where(lane >= HEAD_DIM, q, zero)], axis=0)


def _unstack_heads(acc):
    r = acc.shape[0] // 2
    lane = lax.broadcasted_iota(jnp.int32, (r, PAIR), 1)
    return jnp.where(lane < HEAD_DIM, acc[:r], acc[r:])


def _sb_scores(z, mask):
    sp = jnp.maximum(z, 0.0) + jnp.log2(1.0 + jnp.exp2(-jnp.abs(z)))
    ls = z - sp
    if mask is not None:
        sp = jnp.where(mask, sp, 0.0)
        ls = jnp.where(mask, ls, MASKED)
    return ls, sp.astype(BF16)


def _sb_weights(ls, spb, u, r):
    cs = jnp.dot(spb, u, preferred_element_type=F32)
    x = ls - cs
    total = cs[:, 0:1] + spb[:, 0:1].astype(F32)
    mass = jnp.broadcast_to(total, (ls.shape[0], PAIR))
    if r is not None:
        x = x - (r if ls.shape[1] == PAIR else jnp.concatenate([r] * (ls.shape[1] // PAIR), axis=1))
        mass = mass + r
    return jnp.exp2(x).astype(BF16), mass


def _sb_first(z_d, v_d, mask_d, u_d, z_p=None, v_p=None, u_p=None):
    ls_d, spb_d = _sb_scores(z_d, mask_d)
    if z_p is not None:
        ls_p, spb_p = _sb_scores(z_p, None)
    a_d, r = _sb_weights(ls_d, spb_d, u_d, None)
    acc = jnp.dot(a_d, v_d, preferred_element_type=F32)
    if z_p is not None:
        a_p, r = _sb_weights(ls_p, spb_p, u_p, r)
        acc = acc + jnp.dot(a_p, v_p, preferred_element_type=F32)
    return r, acc


def _sb_block(z, v, u, mask, r_ref, acc_ref):
    ls, spb = _sb_scores(z, mask)
    a, r = _sb_weights(ls, spb, u, r_ref[...])
    acc_ref[...] += jnp.dot(a, v, preferred_element_type=F32)
    r_ref[...] = r
    return jnp.min(r)


def _causal_mask(rows_per_head, width):
    row = lax.broadcasted_iota(jnp.int32, (2 * rows_per_head, width), 0)
    col = lax.broadcasted_iota(jnp.int32, (2 * rows_per_head, width), 1)
    return col < jnp.where(row >= rows_per_head, row - rows_per_head, row)


def _meta_mask(rows, width):
    return lax.broadcasted_iota(jnp.int32, (rows, width), 1) < N_META


def _attn_prompt_kernel(q_ref, kt_ref, vt_ref, u256_ref, u128_ref, o_ref,
                        kt_scr, v_scr, kmt_scr, vm_scr, r_scr, acc_scr):
    g = pl.program_id(2)
    tq = KBLK

    @pl.when(g == 0)
    def _():
        for c in range(v_scr.shape[0] // KBLK):
            cols = slice(N_META + c * KBLK, N_META + (c + 1) * KBLK)
            kt_scr[:, c * KBLK:(c + 1) * KBLK] = kt_ref[0, :, cols].astype(BF16)
            v_scr[c * KBLK:(c + 1) * KBLK, :] = vt_ref[0, :, cols].T.astype(BF16)
        kmt_scr[...] = kt_ref[0, :, 0:MBLK].astype(BF16)
        vm_scr[...] = vt_ref[0, :, 0:MBLK].T.astype(BF16)

    u256 = u256_ref[...]
    n_sub = q_ref.shape[1] // tq
    qs = [_stack_heads(q_ref[0, s * tq:(s + 1) * tq, :]) for s in range(n_sub)]

    def scores(s, j):
        return jnp.dot(qs[s], kt_scr[:, pl.ds(pl.multiple_of(j * tq, tq), tq)], preferred_element_type=F32)

    def values(j):
        return v_scr[pl.ds(pl.multiple_of(j * tq, tq), tq), :]

    def first_blocks(first_has_past):
        mask = _causal_mask(tq, tq)
        for s in range(n_sub):
            qi = g * n_sub + s
            if s > 0 or first_has_past:
                r, acc = _sb_first(scores(s, qi), values(qi), mask, u256, scores(s, qi - 1), values(qi - 1), u256)
            else:
                r, acc = _sb_first(scores(s, qi), values(qi), mask, u256)
            r_scr[s] = r
            acc_scr[s] = acc

    @pl.when(g == 0)
    def _():
        first_blocks(False)

    @pl.when(g > 0)
    def _():
        first_blocks(True)

    r_all = r_scr[0]
    for s in range(1, n_sub):
        r_all = jnp.minimum(r_all, r_scr[s])

    @pl.when(jnp.min(r_all) < R_DONE)
    def _():
        for s in range(n_sub):
            qi = g * n_sub + s
            r_ref, acc_ref = r_scr.at[s], acc_scr.at[s]

            def cond(c, qi=qi):
                return jnp.logical_and(c[0] < qi - 1, c[1] < R_DONE)

            def body(c, s=s, qi=qi, r_ref=r_ref, acc_ref=acc_ref):
                j = qi - 2 - c[0]
                return c[0] + 1, _sb_block(scores(s, j), values(j), u256, None, r_ref, acc_ref)

            _, rmin = lax.while_loop(cond, body, (jnp.int32(0), jnp.min(r_ref[...])))

            @pl.when(rmin < R_DONE)
            def _(s=s, r_ref=r_ref, acc_ref=acc_ref):
                z = jnp.dot(qs[s], kmt_scr[...], preferred_element_type=F32)
                _sb_block(z, vm_scr[...], u128_ref[...], _meta_mask(2 * tq, MBLK), r_ref, acc_ref)

    for s in range(n_sub):
        o_ref[0, s * tq:(s + 1) * tq, :] = _unstack_heads(acc_scr[s]).astype(BF16)


def _suffix_sum_matrix(w):
    j = lax.broadcasted_iota(jnp.int32, (w, w), 0)
    s = lax.broadcasted_iota(jnp.int32, (w, w), 1)
    return (j > s).astype(BF16)


def _attn_prompt(q, kt, vt):
    b, t, d = q.shape
    tq = Q_SUB * KBLK
    const2 = lambda bi, hp, g: (0, 0)
    kv_spec = pl.BlockSpec((1, PAIR, N_META + t), lambda bi, hp, g: (bi, hp, 0))
    return pl.pallas_call(
        _attn_prompt_kernel,
        grid=(b, d // PAIR, t // tq),
        in_specs=[
            pl.BlockSpec((1, tq, PAIR), lambda bi, hp, g: (bi, g, hp)),
            kv_spec, kv_spec,
            pl.BlockSpec((KBLK, KBLK), const2),
            pl.BlockSpec((MBLK, MBLK), const2),
        ],
        out_specs=pl.BlockSpec((1, tq, PAIR), lambda bi, hp, g: (bi, g, hp)),
        out_shape=jax.ShapeDtypeStruct((b, t, d), BF16),
        scratch_shapes=[pltpu.VMEM((PAIR, t), BF16), pltpu.VMEM((t, PAIR), BF16),
                        pltpu.VMEM((PAIR, MBLK), BF16), pltpu.VMEM((MBLK, PAIR), BF16),
                        pltpu.VMEM((Q_SUB, 2 * KBLK, PAIR), F32), pltpu.VMEM((Q_SUB, 2 * KBLK, PAIR), F32)],
        compiler_params=pltpu.CompilerParams(
            dimension_semantics=("arbitrary", "arbitrary", "arbitrary"), vmem_limit_bytes=VMEM_LIMIT),
        name="attn_prompt",
    )(q, kt, vt, _suffix_sum_matrix(KBLK), _suffix_sum_matrix(MBLK))


def _attn_sample_kernel(q_ref, kn_ref, vn_ref, ck_ref, cv_ref, u256_ref, u128_ref, o_ref, r_ref, acc_ref,
                        *, s_len, n_pairs):
    u256 = u256_ref[...]
    u128 = u128_ref[...]
    nt = (((1,), (1,)), ((), ()))
    lanes = [slice(p * PAIR, (p + 1) * PAIR) for p in range(n_pairs)]
    qs = [_stack_heads(q_ref[0, :, ln]) for ln in lanes]

    def cache_block(p, j):
        cols = slice(j * KBLK, (j + 1) * KBLK)
        z = jnp.dot(qs[p], ck_ref[0, lanes[p], cols].astype(BF16), preferred_element_type=F32)
        return z, cv_ref[0, lanes[p], cols].T.astype(BF16)

    mask = _causal_mask(s_len, MBLK)
    r_all = None
    for p in range(n_pairs):
        z_new = lax.dot_general(qs[p], kn_ref[0, :, lanes[p]], nt, preferred_element_type=F32)
        z_past, v_past = cache_block(p, RECENT_BLOCKS - 1)
        r, acc = _sb_first(z_new, vn_ref[0, :, lanes[p]], mask, u128, z_past, v_past, u256)
        r_ref[0, p] = r
        acc_ref[0, p] = acc
        r_all = r if r_all is None else jnp.minimum(r_all, r)

    @pl.when(jnp.min(r_all) < R_DONE)
    def _():
        for p in range(n_pairs):
            for j in range(RECENT_BLOCKS - 2, -1, -1):
                z, v = cache_block(p, j)
                _sb_block(z, v, u256, None, r_ref.at[0, p], acc_ref.at[0, p])

    for p in range(n_pairs):
        o_ref[0, :, lanes[p]] = _unstack_heads(acc_ref[0, p]).astype(BF16)


def _attn_sample_rest_kernel(q_ref, ck_ref, cv_ref, km_ref, vm_ref, r_in_ref, acc_in_ref, u256_ref, u128_ref,
                             o_ref, r_scr, acc_scr, *, s_len, n_rest):
    qs = _stack_heads(q_ref[0])
    u256 = u256_ref[...]
    nt = (((1,), (1,)), ((), ()))
    r_scr[...] = r_in_ref[0, 0]
    acc_scr[...] = acc_in_ref[0, 0]

    def cond(c):
        return jnp.logical_and(c[0] < n_rest, c[1] < R_DONE)

    def body(c):
        start = pl.multiple_of((n_rest - 1 - c[0]) * KBLK, KBLK)
        z = jnp.dot(qs, ck_ref[0, :, pl.ds(start, KBLK)].astype(BF16), preferred_element_type=F32)
        v = cv_ref[0, :, pl.ds(start, KBLK)].T.astype(BF16)
        return c[0] + 1, _sb_block(z, v, u256, None, r_scr, acc_scr)

    _, rmin = lax.while_loop(cond, body, (jnp.int32(0), jnp.min(r_scr[...])))

    @pl.when(rmin < R_DONE)
    def _():
        z = lax.dot_general(qs, km_ref[...], nt, preferred_element_type=F32)
        _sb_block(z, vm_ref[...], u128_ref[...], _meta_mask(2 * s_len, MBLK), r_scr, acc_scr)

    o_ref[0] = _unstack_heads(acc_scr[...]).astype(BF16)


def _attn_sample(q, k_new, v_new, cache_k, cache_v, k_meta, v_meta):
    b, s_len, d = q.shape
    p = cache_k.shape[2]
    n_pairs = d // PAIR
    n_groups = d // (n_pairs * PAIR)
    recent = RECENT_BLOCKS * KBLK
    grp = lambda bi, h: (bi, 0, h)
    const2 = lambda bi, h: (0, 0)
    state = jax.ShapeDtypeStruct((b, d // PAIR, 2 * s_len, PAIR), F32)
    state_spec = pl.BlockSpec((1, n_pairs, 2 * s_len, PAIR), lambda bi, h: (bi, h, 0, 0))
    recent_spec = pl.BlockSpec((1, n_pairs * PAIR, recent), lambda bi, h: (bi, h, p // recent - 1))
    u256, u128 = _suffix_sum_matrix(KBLK), _suffix_sum_matrix(MBLK)
    o_recent, r_state, acc_state = pl.pallas_call(
        functools.partial(_attn_sample_kernel, s_len=s_len, n_pairs=n_pairs),
        grid=(b, n_groups),
        in_specs=[
            pl.BlockSpec((1, s_len, n_pairs * PAIR), grp),
            pl.BlockSpec((1, MBLK, n_pairs * PAIR), grp), pl.BlockSpec((1, MBLK, n_pairs * PAIR), grp),
            recent_spec, recent_spec,
            pl.BlockSpec((KBLK, KBLK), const2), pl.BlockSpec((MBLK, MBLK), const2),
        ],
        out_specs=[pl.BlockSpec((1, s_len, n_pairs * PAIR), grp), state_spec, state_spec],
        out_shape=[jax.ShapeDtypeStruct((b, s_len, d), BF16), state, state],
        compiler_params=pltpu.CompilerParams(
            dimension_semantics=("arbitrary", "arbitrary"), vmem_limit_bytes=VMEM_LIMIT),
        name="attn_sample",
    )(q, k_new, v_new, cache_k, cache_v, u256, u128)

    def rest():
        pair = lambda bi, hp: (bi, 0, hp)
        one_state = pl.BlockSpec((1, 1, 2 * s_len, PAIR), lambda bi, hp: (bi, hp, 0, 0))
        cache_spec = pl.BlockSpec((1, PAIR, p), lambda bi, hp: (bi, hp, 0))
        return pl.pallas_call(
            functools.partial(_attn_sample_rest_kernel, s_len=s_len, n_rest=p // KBLK - RECENT_BLOCKS),
            grid=(b, d // PAIR),
            in_specs=[
                pl.BlockSpec((1, s_len, PAIR), pair),
                cache_spec, cache_spec,
                pl.BlockSpec((MBLK, PAIR), lambda bi, hp: (0, hp)),
                pl.BlockSpec((MBLK, PAIR), lambda bi, hp: (0, hp)),
                one_state, one_state,
                pl.BlockSpec((KBLK, KBLK), const2), pl.BlockSpec((MBLK, MBLK), const2),
            ],
            out_specs=pl.BlockSpec((1, s_len, PAIR), pair),
            out_shape=jax.ShapeDtypeStruct((b, s_len, d), BF16),
            scratch_shapes=[pltpu.VMEM((2 * s_len, PAIR), F32), pltpu.VMEM((2 * s_len, PAIR), F32)],
            compiler_params=pltpu.CompilerParams(
                dimension_semantics=("arbitrary", "arbitrary"), vmem_limit_bytes=VMEM_LIMIT),
            name="attn_sample_rest",
        )(q, cache_k, cache_v, k_meta, v_meta, r_state, acc_state, u256, u128)

    return lax.cond(jnp.min(r_state) < R_DONE, rest, lambda: o_recent)


def _merge_mlp_kernel(x_ref, conv_ref, attn_ref, gates_ref, wout_ref, gmlp_ref, wup_ref, wdown_ref, gfin_ref,
                      y_ref):
    g = gates_ref[...]
    mixed = (g[:, :D_MODEL].astype(F32) * conv_ref[...].astype(F32)
             + g[:, D_MODEL:].astype(F32) * attn_ref[...].astype(F32))
    x1 = x_ref[...] + jnp.dot(mixed.astype(BF16), wout_ref[...], preferred_element_type=F32)
    h = _rms(x1, gmlp_ref[...]).astype(BF16)
    acc = x1
    for c in range(D_FF // D_MODEL):
        cs = slice(c * D_MODEL, (c + 1) * D_MODEL)
        u = jnp.maximum(jnp.dot(h, wup_ref[:, cs], preferred_element_type=F32), 0.0)
        acc = acc + jnp.dot((u * u).astype(BF16), wdown_ref[cs, :], preferred_element_type=F32)
    y_ref[...] = _rms(acc, gfin_ref[...])


def _merge_mlp(x, conv, attn, gates, w_out, g_mlp, w_up, w_down, g_final, tm):
    m, d = x.shape
    row = lambda i: (i, 0)
    const = lambda i: (0, 0)
    resident = functools.partial(pl.BlockSpec, index_map=const, pipeline_mode=pl.Buffered(1))
    return pl.pallas_call(
        _merge_mlp_kernel,
        grid=(m // tm,),
        in_specs=[
            pl.BlockSpec((tm, d), row), pl.BlockSpec((tm, d), row), pl.BlockSpec((tm, d), row),
            pl.BlockSpec((tm, 2 * d), row),
            resident((d, d)), resident((1, d)), resident((d, D_FF)), resident((D_FF, d)), resident((1, d)),
        ],
        out_specs=pl.BlockSpec((tm, d), row),
        out_shape=jax.ShapeDtypeStruct((m, d), F32),
        compiler_params=pltpu.CompilerParams(
            dimension_semantics=("arbitrary",), vmem_limit_bytes=VMEM_LIMIT),
        name="merge_mlp",
    )(x, conv, attn, gates, w_out, g_mlp.reshape(1, d), w_up, w_down, g_final.reshape(1, d))


def kernel(x_prompt, x_sample, cache_k, cache_v, cache_conv, meta, g_mix, w_in, w_dw, b_dw, g_ln_conv,
           b_ln_conv, w_pw2, w_out, g_mlp, w_up, w_down, g_final):
    b, t, d = x_prompt.shape
    sb, s_len, _ = x_sample.shape
    depth, _, past, _, _ = cache_k.shape
    assert depth == 1 and d == D_MODEL and meta.shape == (N_META, d)
    assert t % (Q_SUB * KBLK) == 0 and past % (RECENT_BLOCKS * KBLK) == 0 and s_len % 16 == 0 and CONV_K - 1 <= s_len <= MBLK
    n_s = sb * s_len

    w_in_b = w_in[0].astype(BF16)
    w_pw2_b = w_pw2[0].astype(BF16)
    w_out_b = w_out[0].astype(BF16)
    w_up_b = w_up[0].astype(BF16)
    w_down_b = w_down[0].astype(BF16)

    x_sm = jnp.concatenate([x_sample.reshape(n_s, d), meta.astype(F32)], axis=0)
    glu_s, q_s, k_s, kb_s, v_s, vb_s, gates_s = _in_proj(x_sm, g_mix[0], w_in_b, tm=n_s + N_META)

    pad_meta = ((0, MBLK - N_META), (0, 0))
    kb_meta = jnp.pad(kb_s[n_s:], pad_meta)
    vb_meta = jnp.pad(vb_s[n_s:], pad_meta)

    conv_w = (w_dw[0], b_dw[0], g_ln_conv[0], b_ln_conv[0], w_pw2_b)
    w_main_b = jnp.concatenate([w_in_b[:, :3 * d], w_in_b[:, 5 * d:]], axis=1)
    w_kvt_b = _transposed_columns(w_in[0], 3 * d, 2 * d)
    head_p = jnp.pad(glu_s[n_s:], ((HALO - N_META, 0), (0, 0)))[None]
    q_p, gates_p, conv_p, kt_p, vt_p, glu_tail_p = _in_proj_prompt(
        x_prompt, meta.astype(F32), head_p, g_mix[0], w_main_b, w_kvt_b, *conv_w, tm=TM)

    glu_s3 = glu_s[:n_s].reshape(sb, s_len, d)
    head_s = jnp.pad(cache_conv[0], ((0, 0), (HALO - (CONV_K - 1), 0), (0, 0)))
    conv_s = _conv_branch(glu_s3, head_s, *conv_w)

    attn_p = _attn_prompt(q_p, kt_p, vt_p)
    pad_new = ((0, 0), (0, MBLK - s_len), (0, 0))
    feature_major = lambda c: c.transpose(0, 2, 3, 1).reshape(sb, d, past)
    attn_s = _attn_sample(
        q_s[:n_s].reshape(sb, s_len, d),
        jnp.pad(kb_s[:n_s].reshape(sb, s_len, d), pad_new), jnp.pad(vb_s[:n_s].reshape(sb, s_len, d), pad_new),
        feature_major(cache_k[0]), feature_major(cache_v[0]), kb_meta, vb_meta)

    mlp_w = (w_out_b, g_mlp[0], w_up_b, w_down_b, g_final)
    y_p = _merge_mlp(x_prompt.reshape(b * t, d), conv_p.reshape(b * t, d), attn_p.reshape(b * t, d),
                     gates_p.reshape(b * t, 2 * d), *mlp_w, tm=TM)
    y_s = _merge_mlp(x_sample.reshape(n_s, d), conv_s, attn_s.reshape(n_s, d), gates_s[:n_s],
                     *mlp_w, tm=n_s)

    def key_major(xt):
        return xt.reshape(1, b, N_HEADS, HEAD_DIM, N_META + t).transpose(0, 1, 4, 2, 3)

    return (
        y_p.reshape(b, t, d),
        y_s.reshape(sb, s_len, d),
        key_major(kt_p),
        key_major(vt_p),
        glu_tail_p[:, HALO - (CONV_K - 1):][None],
        k_s[:n_s].reshape(1, sb, s_len, N_HEADS, HEAD_DIM),
        v_s[:n_s].reshape(1, sb, s_len, N_HEADS, HEAD_DIM),
        glu_s3[:, s_len - (CONV_K - 1):][None],
    )
```

```python
import functools

import jax
import jax.numpy as jnp
from jax import lax
from jax.experimental import pallas as pl
from jax.experimental.pallas import tpu as pltpu

F32 = jnp.float32
BF16 = jnp.bfloat16

D_MODEL = 1024
N_META = 16
CONV_K = 31
N_HEADS = 16
HEAD_DIM = 64
D_FF = 4 * D_MODEL
EPS = 1e-6
ATTN_SCALE = HEAD_DIM ** -0.5
LOG2E = 1.4426950408889634
LANES = 128
SUBLANES = 8
PAIR = 2 * HEAD_DIM
TM = 512
HALO = 32
KBLK = 256
MBLK = 128
Q_SUB = 8
RECENT_BLOCKS = 1
MASKED = -1e30
R_DONE = 160.0
VMEM_LIMIT = 56 * 1024 * 1024


def _sigmoid(x):
    return 1.0 / (1.0 + jnp.exp(-x))


def _rms(x, g):
    return x * lax.rsqrt(jnp.mean(x * x, axis=-1, keepdims=True) + EPS) * g


def _in_proj_kernel(x_ref, g_ref, w_ref, wb_ref, glu_ref, q_ref, k_ref, kb_ref, v_ref, vb_ref,
                    gates_ref, h_scr):
    j = pl.program_id(1)

    @pl.when(j == 0)
    def _():
        h_scr[...] = _rms(x_ref[...], g_ref[...]).astype(BF16)

    h = h_scr[...]
    p = jnp.dot(h, w_ref[...], preferred_element_type=F32)

    @pl.when(j == 0)
    def _():
        b = jnp.dot(h, wb_ref[...], preferred_element_type=F32)
        glu_ref[...] = p * _sigmoid(b)

    @pl.when(j == 1)
    def _():
        q_ref[...] = (p * (ATTN_SCALE * LOG2E)).astype(BF16)

    @pl.when(j == 2)
    def _():
        k_ref[...] = p
        kb_ref[...] = p.astype(BF16)

    @pl.when(j == 3)
    def _():
        v_ref[...] = p
        vb_ref[...] = p.astype(BF16)

    @pl.when(j == 4)
    def _():
        gates_ref[:, :D_MODEL] = _sigmoid(p).astype(BF16)

    @pl.when(j == 5)
    def _():
        gates_ref[:, D_MODEL:] = _sigmoid(p).astype(BF16)


def _in_proj(x, g, w_in, tm):
    m = x.shape[0]
    d = D_MODEL
    row = lambda i, j: (i, 0)
    return pl.pallas_call(
        _in_proj_kernel,
        grid=(m // tm, 6),
        in_specs=[
            pl.BlockSpec((tm, d), row),
            pl.BlockSpec((1, d), lambda i, j: (0, 0)),
            pl.BlockSpec((d, d), lambda i, j: (0, jnp.where(j == 0, 0, j + 1))),
            pl.BlockSpec((d, d), lambda i, j: (0, 1)),
        ],
        out_specs=[
            pl.BlockSpec((tm, d), row), pl.BlockSpec((tm, d), row),
            pl.BlockSpec((tm, d), row), pl.BlockSpec((tm, d), row),
            pl.BlockSpec((tm, d), row), pl.BlockSpec((tm, d), row),
            pl.BlockSpec((tm, 2 * d), row),
        ],
        out_shape=[
            jax.ShapeDtypeStruct((m, d), F32), jax.ShapeDtypeStruct((m, d), BF16),
            jax.ShapeDtypeStruct((m, d), F32), jax.ShapeDtypeStruct((m, d), BF16),
            jax.ShapeDtypeStruct((m, d), F32), jax.ShapeDtypeStruct((m, d), BF16),
            jax.ShapeDtypeStruct((m, 2 * d), BF16),
        ],
        scratch_shapes=[pltpu.VMEM((tm, d), BF16)],
        compiler_params=pltpu.CompilerParams(
            dimension_semantics=("arbitrary", "arbitrary"), vmem_limit_bytes=VMEM_LIMIT),
        name="in_proj",
    )(x, g.reshape(1, d), w_in, w_in)


def _transpose_kernel(w_ref, o_ref):
    o_ref[...] = w_ref[...].T.astype(BF16)


def _transposed_columns(w, first, count, tn=512):
    k = w.shape[0]
    return pl.pallas_call(
        _transpose_kernel,
        grid=(count // tn,),
        in_specs=[pl.BlockSpec((k, tn), lambda j: (0, first // tn + j))],
        out_specs=pl.BlockSpec((tn, k), lambda j: (j, 0)),
        out_shape=jax.ShapeDtypeStruct((count, k), BF16),
        compiler_params=pltpu.CompilerParams(dimension_semantics=("arbitrary",), vmem_limit_bytes=VMEM_LIMIT),
        name="transpose_weights",
    )(w)


def _depthwise_conv_block(c, ext_scr, sh_scr, y_scr, wdw_ref, bdw_ref, tm):
    rows = HALO + tm - SUBLANES
    rc = min(tm, 64)
    first = HALO - (CONV_K - 1)
    cs = slice(c * LANES, (c + 1) * LANES)
    sh = sh_scr.at[c % 2]
    for m in range(1, SUBLANES):
        sh[m - 1, 0:rows, :] = ext_scr[m:m + rows, cs]
    for r in range(tm // rc):
        acc = jnp.broadcast_to(bdw_ref[:, cs], (rc, LANES))
        for t in range(CONV_K):
            m = (t + first) % SUBLANES
            lo = r * rc + (t + first) - m
            src = ext_scr[lo:lo + rc, cs] if m == 0 else sh[m - 1, lo:lo + rc, :]
            acc = acc + wdw_ref[t:t + 1, cs] * src
        y_scr[r * rc:(r + 1) * rc, cs] = acc


N_CONV_BLOCKS = D_MODEL // LANES


def _ln_swish_pointwise(y, gln_ref, bln_ref, wpw_ref):
    yc = y - jnp.mean(y, axis=-1, keepdims=True)
    var = jnp.mean(yc * yc, axis=-1, keepdims=True)
    ln = yc * lax.rsqrt(var + EPS) * gln_ref[...] + bln_ref[...]
    s = ln * _sigmoid(ln)
    return jnp.dot(s.astype(BF16), wpw_ref[...], preferred_element_type=F32)


def _in_proj_prompt_kernel(x_ref, meta_ref, head_ref, g_ref, w_ref, wt_ref, wdw_ref, bdw_ref, gln_ref, bln_ref,
                           wpw_ref, q_ref, gates_ref, conv_ref, kt_ref, vt_ref, glu_tail_ref,
                           h_scr, hs_scr, carry_scr, ext_scr, sh_scr, y_scr, *, tm, n_tiles):
    t = pl.program_id(1)
    d = D_MODEL
    nt = (((1,), (1,)), ((), ()))

    def keys_values():
        hs_scr[0:N_META, :] = carry_scr[...]
        hs_scr[N_META:tm, :] = h_scr[0:tm - N_META, :]
        carry_scr[...] = h_scr[tm - N_META:tm, :]
        hs = hs_scr[...]
        kt_ref[0] = lax.dot_general(wt_ref[0:d, :], hs, nt, preferred_element_type=F32)
        vt_ref[0] = lax.dot_general(wt_ref[d:2 * d, :], hs, nt, preferred_element_type=F32)

    @pl.when(t == 0)
    def _():
        carry_scr[...] = _rms(meta_ref[...], g_ref[...]).astype(BF16)
        ext_scr[0:HALO, :] = head_ref[0]

    @pl.when(jnp.logical_and(t > 0, t < n_tiles))
    def _():
        ext_scr[0:HALO, :] = ext_scr[tm:tm + HALO, :]

    @pl.when(t < n_tiles)
    def _():
        h_scr[...] = _rms(x_ref[0], g_ref[...]).astype(BF16)
        h = h_scr[...]
        proj = lambda c: jnp.dot(h, w_ref[:, c * d:(c + 1) * d], preferred_element_type=F32)
        conv = lambda c: _depthwise_conv_block(c, ext_scr, sh_scr, y_scr, wdw_ref, bdw_ref, tm)
        ext_scr[HALO:HALO + tm, :] = proj(0) * _sigmoid(proj(1))
        glu_tail_ref[0] = ext_scr[tm:tm + HALO, :]
        for c in range(N_CONV_BLOCKS):
            conv(c)
        q_ref[0] = (proj(2) * (ATTN_SCALE * LOG2E)).astype(BF16)
        gates_ref[0, :, :d] = _sigmoid(proj(3)).astype(BF16)
        gates_ref[0, :, d:] = _sigmoid(proj(4)).astype(BF16)
        keys_values()
        conv_ref[0] = _ln_swish_pointwise(y_scr[...], gln_ref, bln_ref, wpw_ref).astype(BF16)

    @pl.when(t == n_tiles)
    def _():
        keys_values()


def _in_proj_prompt(x, meta, head, g, w_main, w_kvt, w_dw, b_dw, g_ln, b_ln, w_pw2, tm):
    b, t_len, d = x.shape
    n_tiles = t_len // tm
    row = lambda bi, t: (bi, jnp.minimum(t, n_tiles - 1), 0)
    const = lambda bi, t: (0, 0)
    resident = functools.partial(pl.BlockSpec, index_map=const, pipeline_mode=pl.Buffered(1))
    return pl.pallas_call(
        functools.partial(_in_proj_prompt_kernel, tm=tm, n_tiles=n_tiles),
        grid=(b, n_tiles + 1),
        in_specs=[
            pl.BlockSpec((1, tm, d), row),
            resident((N_META, d)),
            pl.BlockSpec((1, HALO, d), lambda bi, t: (0, 0, 0), pipeline_mode=pl.Buffered(1)),
            resident((1, d)), resident((d, 5 * d)), resident((2 * d, d)),
            resident((CONV_K, d)), resident((1, d)), resident((1, d)), resident((1, d)), resident((d, d)),
        ],
        out_specs=[
            pl.BlockSpec((1, tm, d), row), pl.BlockSpec((1, tm, 2 * d), row), pl.BlockSpec((1, tm, d), row),
            pl.BlockSpec((1, d, tm), lambda bi, t: (bi, 0, t)),
            pl.BlockSpec((1, d, tm), lambda bi, t: (bi, 0, t)),
            pl.BlockSpec((1, HALO, d), lambda bi, t: (bi, 0, 0)),
        ],
        out_shape=[
            jax.ShapeDtypeStruct((b, t_len, d), BF16), jax.ShapeDtypeStruct((b, t_len, 2 * d), BF16),
            jax.ShapeDtypeStruct((b, t_len, d), BF16),
            jax.ShapeDtypeStruct((b, d, N_META + t_len), F32), jax.ShapeDtypeStruct((b, d, N_META + t_len), F32),
            jax.ShapeDtypeStruct((b, HALO, d), F32),
        ],
        scratch_shapes=[pltpu.VMEM((tm, d), BF16), pltpu.VMEM((tm, d), BF16), pltpu.VMEM((N_META, d), BF16),
                        pltpu.VMEM((HALO + tm, d), F32), pltpu.VMEM((2, SUBLANES - 1, HALO + tm, LANES), F32),
                        pltpu.VMEM((tm, d), F32)],
        compiler_params=pltpu.CompilerParams(
            dimension_semantics=("arbitrary", "arbitrary"), vmem_limit_bytes=VMEM_LIMIT),
        name="in_proj_prompt",
    )(x, meta, head, g.reshape(1, d), w_main, w_kvt, w_dw, b_dw.reshape(1, d), g_ln.reshape(1, d),
      b_ln.reshape(1, d), w_pw2)


def _conv_kernel(glu_ref, head_ref, wdw_ref, bdw_ref, gln_ref, bln_ref, wpw_ref, out_ref,
                 ext_scr, sh_scr, y_scr, *, tm):
    for i in range(glu_ref.shape[0]):
        ext_scr[i, 0:HALO, :] = head_ref[i]
        ext_scr[i, HALO:HALO + tm, :] = glu_ref[i]
        for c in range(N_CONV_BLOCKS):
            _depthwise_conv_block(c, ext_scr.at[i], sh_scr, y_scr.at[pl.ds(i * tm, tm)], wdw_ref, bdw_ref, tm)
    out_ref[...] = _ln_swish_pointwise(y_scr[...], gln_ref, bln_ref, wpw_ref).astype(BF16)


def _conv_branch(glu, head, w_dw, b_dw, g_ln, b_ln, w_pw2):
    b, tm, d = glu.shape
    const2 = lambda i: (0, 0)
    const3 = lambda i: (0, 0, 0)
    return pl.pallas_call(
        functools.partial(_conv_kernel, tm=tm),
        grid=(1,),
        in_specs=[
            pl.BlockSpec((b, tm, d), const3), pl.BlockSpec((b, HALO, d), const3),
            pl.BlockSpec((CONV_K, d), const2),
            pl.BlockSpec((1, d), const2), pl.BlockSpec((1, d), const2), pl.BlockSpec((1, d), const2),
            pl.BlockSpec((d, d), const2),
        ],
        out_specs=pl.BlockSpec((b * tm, d), const2),
        out_shape=jax.ShapeDtypeStruct((b * tm, d), BF16),
        scratch_shapes=[pltpu.VMEM((b, HALO + tm, d), F32), pltpu.VMEM((2, SUBLANES - 1, HALO + tm, LANES), F32),
                        pltpu.VMEM((b * tm, d), F32)],
        compiler_params=pltpu.CompilerParams(dimension_semantics=("arbitrary",), vmem_limit_bytes=VMEM_LIMIT),
        name="conv_branch",
    )(glu, head, w_dw, b_dw.reshape(1, d), g_ln.reshape(1, d), b_ln.reshape(1, d), w_pw2)


def _stack_heads(q):
    lane = lax.broadcasted_iota(jnp.int32, q.shape, 1)
    zero = jnp.zeros_like(q)
    return jnp.concatenate([jnp.where(lane < HEAD_DIM, q, zero), jnp.where(lane >= HEAD_DIM, q, zero)], axis=0)


def _unstack_heads(acc):
    r = acc.shape[0] // 2
    lane = lax.broadcasted_iota(jnp.int32, (r, PAIR), 1)
    return jnp.where(lane < HEAD_DIM, acc[:r], acc[r:])


def _sb_scores(z, mask):
    sp = jnp.maximum(z, 0.0) + jnp.log2(1.0 + jnp.exp2(-jnp.abs(z)))
    ls = z - sp
    if mask is not None:
        sp = jnp.where(mask, sp, 0.0)
        ls = jnp.where(mask, ls, MASKED)
    return ls, sp.astype(BF16)


def _sb_weights(ls, spb, u, r):
    cs = jnp.dot(spb, u, preferred_element_type=F32)
    x = ls - cs
    total = cs[:, 0:1] + spb[:, 0:1].astype(F32)
    mass = jnp.broadcast_to(total, (ls.shape[0], PAIR))
    if r is not None:
        x = x - (r if ls.shape[1] == PAIR else jnp.concatenate([r] * (ls.shape[1] // PAIR), axis=1))
        mass = mass + r
    return jnp.exp2(x).astype(BF16), mass


def _sb_first(z_d, v_d, mask_d, u_d, z_p=None, v_p=None, u_p=None):
    ls_d, spb_d = _sb_scores(z_d, mask_d)
    if z_p is not None:
        ls_p, spb_p = _sb_scores(z_p, None)
    a_d, r = _sb_weights(ls_d, spb_d, u_d, None)
    acc = jnp.dot(a_d, v_d, preferred_element_type=F32)
    if z_p is not None:
        a_p, r = _sb_weights(ls_p, spb_p, u_p, r)
        acc = acc + jnp.dot(a_p, v_p, preferred_element_type=F32)
    return r, acc


def _sb_block(z, v, u, mask, r_ref, acc_ref):
    ls, spb = _sb_scores(z, mask)
    a, r = _sb_weights(ls, spb, u, r_ref[...])
    acc_ref[...] += jnp.dot(a, v, preferred_element_type=F32)
    r_ref[...] = r
    return jnp.min(r)


def _causal_mask(rows_per_head, width):
    row = lax.broadcasted_iota(jnp.int32, (2 * rows_per_head, width), 0)
    col = lax.broadcasted_iota(jnp.int32, (2 * rows_per_head, width), 1)
    return col < jnp.where(row >= rows_per_head, row - rows_per_head, row)


def _meta_mask(rows, width):
    return lax.broadcasted_iota(jnp.int32, (rows, width), 1) < N_META


def _attn_prompt_kernel(q_ref, kt_ref, vt_ref, u256_ref, u128_ref, o_ref,
                        kt_scr, v_scr, kmt_scr, vm_scr, r_scr, acc_scr):
    g = pl.program_id(2)
    tq = KBLK

    @pl.when(g == 0)
    def _():
        for c in range(v_scr.shape[0] // KBLK):
            cols = slice(N_META + c * KBLK, N_META + (c + 1) * KBLK)
            kt_scr[:, c * KBLK:(c + 1) * KBLK] = kt_ref[0, :, cols].astype(BF16)
            v_scr[c * KBLK:(c + 1) * KBLK, :] = vt_ref[0, :, cols].T.astype(BF16)
        kmt_scr[...] = kt_ref[0, :, 0:MBLK].astype(BF16)
        vm_scr[...] = vt_ref[0, :, 0:MBLK].T.astype(BF16)

    u256 = u256_ref[...]
    n_sub = q_ref.shape[1] // tq
    qs = [_stack_heads(q_ref[0, s * tq:(s + 1) * tq, :]) for s in range(n_sub)]

    def scores(s, j):
        return jnp.dot(qs[s], kt_scr[:, pl.ds(pl.multiple_of(j * tq, tq), tq)], preferred_element_type=F32)

    def values(j):
        return v_scr[pl.ds(pl.multiple_of(j * tq, tq), tq), :]

    def first_blocks(first_has_past):
        mask = _causal_mask(tq, tq)
        for s in range(n_sub):
            qi = g * n_sub + s
            if s > 0 or first_has_past:
                r, acc = _sb_first(scores(s, qi), values(qi), mask, u256, scores(s, qi - 1), values(qi - 1), u256)
            else:
                r, acc = _sb_first(scores(s, qi), values(qi), mask, u256)
            r_scr[s] = r
            acc_scr[s] = acc

    @pl.when(g == 0)
    def _():
        first_blocks(False)

    @pl.when(g > 0)
    def _():
        first_blocks(True)

    r_all = r_scr[0]
    for s in range(1, n_sub):
        r_all = jnp.minimum(r_all, r_scr[s])

    @pl.when(jnp.min(r_all) < R_DONE)
    def _():
        for s in range(n_sub):
            qi = g * n_sub + s
            r_ref, acc_ref = r_scr.at[s], acc_scr.at[s]

            def cond(c, qi=qi):
                return jnp.logical_and(c[0] < qi - 1, c[1] < R_DONE)

            def body(c, s=s, qi=qi, r_ref=r_ref, acc_ref=acc_ref):
                j = qi - 2 - c[0]
                return c[0] + 1, _sb_block(scores(s, j), values(j), u256, None, r_ref, acc_ref)

            _, rmin = lax.while_loop(cond, body, (jnp.int32(0), jnp.min(r_ref[...])))

            @pl.when(rmin < R_DONE)
            def _(s=s, r_ref=r_ref, acc_ref=acc_ref):
                z = jnp.dot(qs[s], kmt_scr[...], preferred_element_type=F32)
                _sb_block(z, vm_scr[...], u128_ref[...], _meta_mask(2 * tq, MBLK), r_ref, acc_ref)

    for s in range(n_sub):
        o_ref[0, s * tq:(s + 1) * tq, :] = _unstack_heads(acc_scr[s]).astype(BF16)


def _suffix_sum_matrix(w):
    j = lax.broadcasted_iota(jnp.int32, (w, w), 0)
    s = lax.broadcasted_iota(jnp.int32, (w, w), 1)
    return (j > s).astype(BF16)


def _attn_prompt(q, kt, vt):
    b, t, d = q.shape
    tq = Q_SUB * KBLK
    const2 = lambda bi, hp, g: (0, 0)
    kv_spec = pl.BlockSpec((1, PAIR, N_META + t), lambda bi, hp, g: (bi, hp, 0))
    return pl.pallas_call(
        _attn_prompt_kernel,
        grid=(b, d // PAIR, t // tq),
        in_specs=[
            pl.BlockSpec((1, tq, PAIR), lambda bi, hp, g: (bi, g, hp)),
            kv_spec, kv_spec,
            pl.BlockSpec((KBLK, KBLK), const2),
            pl.BlockSpec((MBLK, MBLK), const2),
        ],
        out_specs=pl.BlockSpec((1, tq, PAIR), lambda bi, hp, g: (bi, g, hp)),
        out_shape=jax.ShapeDtypeStruct((b, t, d), BF16),
        scratch_shapes=[pltpu.VMEM((PAIR, t), BF16), pltpu.VMEM((t, PAIR), BF16),
                        pltpu.VMEM((PAIR, MBLK), BF16), pltpu.VMEM((MBLK, PAIR), BF16),
                        pltpu.VMEM((Q_SUB, 2 * KBLK, PAIR), F32), pltpu.VMEM((Q_SUB, 2 * KBLK, PAIR), F32)],
        compiler_params=pltpu.CompilerParams(
            dimension_semantics=("arbitrary", "arbitrary", "arbitrary"), vmem_limit_bytes=VMEM_LIMIT),
        name="attn_prompt",
    )(q, kt, vt, _suffix_sum_matrix(KBLK), _suffix_sum_matrix(MBLK))


def _attn_sample_kernel(q_ref, kn_ref, vn_ref, ck_ref, cv_ref, u256_ref, u128_ref, o_ref, r_ref, acc_ref,
                        *, s_len, n_pairs):
    u256 = u256_ref[...]
    u128 = u128_ref[...]
    nt = (((1,), (1,)), ((), ()))
    lanes = [slice(p * PAIR, (p + 1) * PAIR) for p in range(n_pairs)]
    qs = [_stack_heads(q_ref[0, :, ln]) for ln in lanes]

    def cache_block(p, j):
        cols = slice(j * KBLK, (j + 1) * KBLK)
        z = jnp.dot(qs[p], ck_ref[0, lanes[p], cols].astype(BF16), preferred_element_type=F32)
        return z, cv_ref[0, lanes[p], cols].T.astype(BF16)

    mask = _causal_mask(s_len, MBLK)
    r_all = None
    for p in range(n_pairs):
        z_new = lax.dot_general(qs[p], kn_ref[0, :, lanes[p]], nt, preferred_element_type=F32)
        z_past, v_past = cache_block(p, RECENT_BLOCKS - 1)
        r, acc = _sb_first(z_new, vn_ref[0, :, lanes[p]], mask, u128, z_past, v_past, u256)
        r_ref[0, p] = r
        acc_ref[0, p] = acc
        r_all = r if r_all is None else jnp.minimum(r_all, r)

    @pl.when(jnp.min(r_all) < R_DONE)
    def _():
        for p in range(n_pairs):
            for j in range(RECENT_BLOCKS - 2, -1, -1):
                z, v = cache_block(p, j)
                _sb_block(z, v, u256, None, r_ref.at[0, p], acc_ref.at[0, p])

    for p in range(n_pairs):
        o_ref[0, :, lanes[p]] = _unstack_heads(acc_ref[0, p]).astype(BF16)


def _attn_sample_rest_kernel(q_ref, ck_ref, cv_ref, km_ref, vm_ref, r_in_ref, acc_in_ref, u256_ref, u128_ref,
                             o_ref, r_scr, acc_scr, *, s_len, n_rest):
    qs = _stack_heads(q_ref[0])
    u256 = u256_ref[...]
    nt = (((1,), (1,)), ((), ()))
    r_scr[...] = r_in_ref[0, 0]
    acc_scr[...] = acc_in_ref[0, 0]

    def cond(c):
        return jnp.logical_and(c[0] < n_rest, c[1] < R_DONE)

    def body(c):
        start = pl.multiple_of((n_rest - 1 - c[0]) * KBLK, KBLK)
        z = jnp.dot(qs, ck_ref[0, :, pl.ds(start, KBLK)].astype(BF16), preferred_element_type=F32)
        v = cv_ref[0, :, pl.ds(start, KBLK)].T.astype(BF16)
        return c[0] + 1, _sb_block(z, v, u256, None, r_scr, acc_scr)

    _, rmin = lax.while_loop(cond, body, (jnp.int32(0), jnp.min(r_scr[...])))

    @pl.when(rmin < R_DONE)
    def _():
        z = lax.dot_general(qs, km_ref[...], nt, preferred_element_type=F32)
        _sb_block(z, vm_ref[...], u128_ref[...], _meta_mask(2 * s_len, MBLK), r_scr, acc_scr)

    o_ref[0] = _unstack_heads(acc_scr[...]).astype(BF16)


def _attn_sample(q, k_new, v_new, cache_k, cache_v, k_meta, v_meta):
    b, s_len, d = q.shape
    p = cache_k.shape[2]
    n_pairs = d // PAIR
    n_groups = d // (n_pairs * PAIR)
    recent = RECENT_BLOCKS * KBLK
    grp = lambda bi, h: (bi, 0, h)
    const2 = lambda bi, h: (0, 0)
    state = jax.ShapeDtypeStruct((b, d // PAIR, 2 * s_len, PAIR), F32)
    state_spec = pl.BlockSpec((1, n_pairs, 2 * s_len, PAIR), lambda bi, h: (bi, h, 0, 0))
    recent_spec = pl.BlockSpec((1, n_pairs * PAIR, recent), lambda bi, h: (bi, h, p // recent - 1))
    u256, u128 = _suffix_sum_matrix(KBLK), _suffix_sum_matrix(MBLK)
    o_recent, r_state, acc_state = pl.pallas_call(
        functools.partial(_attn_sample_kernel, s_len=s_len, n_pairs=n_pairs),
        grid=(b, n_groups),
        in_specs=[
            pl.BlockSpec((1, s_len, n_pairs * PAIR), grp),
            pl.BlockSpec((1, MBLK, n_pairs * PAIR), grp), pl.BlockSpec((1, MBLK, n_pairs * PAIR), grp),
            recent_spec, recent_spec,
            pl.BlockSpec((KBLK, KBLK), const2), pl.BlockSpec((MBLK, MBLK), const2),
        ],
        out_specs=[pl.BlockSpec((1, s_len, n_pairs * PAIR), grp), state_spec, state_spec],
        out_shape=[jax.ShapeDtypeStruct((b, s_len, d), BF16), state, state],
        compiler_params=pltpu.CompilerParams(
            dimension_semantics=("arbitrary", "arbitrary"), vmem_limit_bytes=VMEM_LIMIT),
        name="attn_sample",
    )(q, k_new, v_new, cache_k, cache_v, u256, u128)

    def rest():
        pair = lambda bi, hp: (bi, 0, hp)
        one_state = pl.BlockSpec((1, 1, 2 * s_len, PAIR), lambda bi, hp: (bi, hp, 0, 0))
        cache_spec = pl.BlockSpec((1, PAIR, p), lambda bi, hp: (bi, hp, 0))
        return pl.pallas_call(
            functools.partial(_attn_sample_rest_kernel, s_len=s_len, n_rest=p // KBLK - RECENT_BLOCKS),
            grid=(b, d // PAIR),
            in_specs=[
                pl.BlockSpec((1, s_len, PAIR), pair),
                cache_spec, cache_spec,
                pl.BlockSpec((MBLK, PAIR), lambda bi, hp: (0, hp)),
                pl.BlockSpec((MBLK, PAIR), lambda bi, hp: (0, hp)),
                one_state, one_state,
                pl.BlockSpec((KBLK, KBLK), const2), pl.BlockSpec((MBLK, MBLK), const2),
            ],
            out_specs=pl.BlockSpec((1, s_len, PAIR), pair),
            out_shape=jax.ShapeDtypeStruct((b, s_len, d), BF16),
            scratch_shapes=[pltpu.VMEM((2 * s_len, PAIR), F32), pltpu.VMEM((2 * s_len, PAIR), F32)],
            compiler_params=pltpu.CompilerParams(
                dimension_semantics=("arbitrary", "arbitrary"), vmem_limit_bytes=VMEM_LIMIT),
            name="attn_sample_rest",
        )(q, cache_k, cache_v, k_meta, v_meta, r_state, acc_state, u256, u128)

    return lax.cond(jnp.min(r_state) < R_DONE, rest, lambda: o_recent)


def _merge_mlp_kernel(x_ref, conv_ref, attn_ref, gates_ref, wout_ref, gmlp_ref, wup_ref, wdown_ref, gfin_ref,
                      y_ref):
    g = gates_ref[...]
    mixed = (g[:, :D_MODEL].astype(F32) * conv_ref[...].astype(F32)
             + g[:, D_MODEL:].astype(F32) * attn_ref[...].astype(F32))
    x1 = x_ref[...] + jnp.dot(mixed.astype(BF16), wout_ref[...], preferred_element_type=F32)
    h = _rms(x1, gmlp_ref[...]).astype(BF16)
    acc = x1
    for c in range(D_FF // D_MODEL):
        cs = slice(c * D_MODEL, (c + 1) * D_MODEL)
        u = jnp.maximum(jnp.dot(h, wup_ref[:, cs], preferred_element_type=F32), 0.0)
        acc = acc + jnp.dot((u * u).astype(BF16), wdown_ref[cs, :], preferred_element_type=F32)
    y_ref[...] = _rms(acc, gfin_ref[...])


def _merge_mlp(x, conv, attn, gates, w_out, g_mlp, w_up, w_down, g_final, tm):
    m, d = x.shape
    row = lambda i: (i, 0)
    const = lambda i: (0, 0)
    resident = functools.partial(pl.BlockSpec, index_map=const, pipeline_mode=pl.Buffered(1))
    return pl.pallas_call(
        _merge_mlp_kernel,
        grid=(m // tm,),
        in_specs=[
            pl.BlockSpec((tm, d), row), pl.BlockSpec((tm, d), row), pl.BlockSpec((tm, d), row),
            pl.BlockSpec((tm, 2 * d), row),
            resident((d, d)), resident((1, d)), resident((d, D_FF)), resident((D_FF, d)), resident((1, d)),
        ],
        out_specs=pl.BlockSpec((tm, d), row),
        out_shape=jax.ShapeDtypeStruct((m, d), F32),
        compiler_params=pltpu.CompilerParams(
            dimension_semantics=("arbitrary",), vmem_limit_bytes=VMEM_LIMIT),
        name="merge_mlp",
    )(x, conv, attn, gates, w_out, g_mlp.reshape(1, d), w_up, w_down, g_final.reshape(1, d))


def kernel(x_prompt, x_sample, cache_k, cache_v, cache_conv, meta, g_mix, w_in, w_dw, b_dw, g_ln_conv,
           b_ln_conv, w_pw2, w_out, g_mlp, w_up, w_down, g_final):
    b, t, d = x_prompt.shape
    sb, s_len, _ = x_sample.shape
    depth, _, past, _, _ = cache_k.shape
    assert depth == 1 and d == D_MODEL and meta.shape == (N_META, d)
    assert t % (Q_SUB * KBLK) == 0 and past % (RECENT_BLOCKS * KBLK) == 0 and s_len % 16 == 0 and CONV_K - 1 <= s_len <= MBLK
    n_s = sb * s_len

    w_in_b = w_in[0].astype(BF16)
    w_pw2_b = w_pw2[0].astype(BF16)
    w_out_b = w_out[0].astype(BF16)
    w_up_b = w_up[0].astype(BF16)
    w_down_b = w_down[0].astype(BF16)

    x_sm = jnp.concatenate([x_sample.reshape(n_s, d), meta.astype(F32)], axis=0)
    glu_s, q_s, k_s, kb_s, v_s, vb_s, gates_s = _in_proj(x_sm, g_mix[0], w_in_b, tm=n_s + N_META)

    pad_meta = ((0, MBLK - N_META), (0, 0))
    kb_meta = jnp.pad(kb_s[n_s:], pad_meta)
    vb_meta = jnp.pad(vb_s[n_s:], pad_meta)

    conv_w = (w_dw[0], b_dw[0], g_ln_conv[0], b_ln_conv[0], w_pw2_b)
    w_main_b = jnp.concatenate([w_in_b[:, :3 * d], w_in_b[:, 5 * d:]], axis=1)
    w_kvt_b = _transposed_columns(w_in[0], 3 * d, 2 * d)
    head_p = jnp.pad(glu_s[n_s:], ((HALO - N_META, 0), (0, 0)))[None]
    q_p, gates_p, conv_p, kt_p, vt_p, glu_tail_p = _in_proj_prompt(
        x_prompt, meta.astype(F32), head_p, g_mix[0], w_main_b, w_kvt_b, *conv_w, tm=TM)

    glu_s3 = glu_s[:n_s].reshape(sb, s_len, d)
    head_s = jnp.pad(cache_conv[0], ((0, 0), (HALO - (CONV_K - 1), 0), (0, 0)))
    conv_s = _conv_branch(glu_s3, head_s, *conv_w)

    attn_p = _attn_prompt(q_p, kt_p, vt_p)
    pad_new = ((0, 0), (0, MBLK - s_len), (0, 0))
    feature_major = lambda c: c.transpose(0, 2, 3, 1).reshape(sb, d, past)
    attn_s = _attn_sample(
        q_s[:n_s].reshape(sb, s_len, d),
        jnp.pad(kb_s[:n_s].reshape(sb, s_len, d), pad_new), jnp.pad(vb_s[:n_s].reshape(sb, s_len, d), pad_new),
        feature_major(cache_k[0]), feature_major(cache_v[0]), kb_meta, vb_meta)

    mlp_w = (w_out_b, g_mlp[0], w_up_b, w_down_b, g_final)
    y_p = _merge_mlp(x_prompt.reshape(b * t, d), conv_p.reshape(b * t, d), attn_p.reshape(b * t, d),
                     gates_p.reshape(b * t, 2 * d), *mlp_w, tm=TM)
    y_s = _merge_mlp(x_sample.reshape(n_s, d), conv_s, attn_s.reshape(n_s, d), gates_s[:n_s],
                     *mlp_w, tm=n_s)

    def key_major(xt):
        return xt.reshape(1, b, N_HEADS, HEAD_DIM, N_META + t).transpose(0, 1, 4, 2, 3)

    return (
        y_p.reshape(b, t, d),
        y_s.reshape(sb, s_len, d),
        key_major(kt_p),
        key_major(vt_p),
        glu_tail_p[:, HALO - (CONV_K - 1):][None],
        k_s[:n_s].reshape(1, sb, s_len, N_HEADS, HEAD_DIM),
        v_s[:n_s].reshape(1, sb, s_len, N_HEADS, HEAD_DIM),
        glu_s3[:, s_len - (CONV_K - 1):][None],
    )
```
